```python
import jax, jax.numpy as jnp
from jax import lax
import numpy as np

D_MODEL = 1024
BATCH = 2
SEQ = 8192
DEPTH = 2

CHUNK = 64
SUB_CHUNK = 16
POOL_WIDTH = 512
POOL_GROUPS = 4
POOL_WINDOWS = (2, 4, 8, 16)
POOL_GROUP_DIM = POOL_WIDTH // POOL_GROUPS
HGRN_HEADS = 4
HGRN_EXPAND = 128
HGRN_HEAD_DIM = 128
HGRN_FORGET = HGRN_HEADS * HGRN_EXPAND
HGRN_INPUT = HGRN_HEADS * HGRN_HEAD_DIM
N_BRANCHES = 2
D_IN = POOL_WIDTH + 2 * HGRN_FORGET + 2 * HGRN_INPUT + N_BRANCHES * D_MODEL
D_FF = 2816
EPS = 1e-6

kernel_name = 'hybrid_pool_hgrn2_macaron_encoder'


def rmsnorm(x, g):
    xf = x.astype(jnp.float32)
    y = xf * lax.rsqrt(jnp.mean(xf * xf, axis=-1, keepdims=True) + EPS)
    return (y * g.astype(jnp.float32)).astype(x.dtype)


def swiglu(x, w_gate, w_up, w_down):
    return (jax.nn.silu(x @ w_gate) * (x @ w_up)) @ w_down


def causal_pool_mixer(u, pool_w, pool_scale):
    b_, s_, _ = u.shape
    ug = u.reshape(b_, s_, POOL_GROUPS, POOL_GROUP_DIM)
    cs = jnp.cumsum(ug.astype(jnp.float32), axis=1)
    cs = jnp.pad(cs, ((0, 0), (1, 0), (0, 0), (0, 0)))
    pos = jnp.arange(1, s_ + 1)
    means = []
    for g, w in enumerate(POOL_WINDOWS):
        lo = jnp.maximum(pos - w, 0)
        window_sum = cs[:, 1:, g] - cs[:, lo, g]
        count = (pos - lo).astype(jnp.float32)
        means.append(window_sum / count[None, :, None])
    pooled = jnp.stack(means, axis=2).astype(u.dtype)
    mixed = jnp.einsum('bsgc,gcd->bsgd', pooled - ug, pool_w)
    return mixed.reshape(b_, s_, POOL_WIDTH) * pool_scale


def hgrn2_chunkwise(q, k, v, log_f):
    b_, h_, s_, dk = q.shape
    dv = v.shape[-1]
    n = s_ // CHUNK
    ns = CHUNK // SUB_CHUNK
    qc = q.reshape(b_, h_, n, CHUNK, dk)
    kc = k.reshape(b_, h_, n, CHUNK, dk)
    vc = v.reshape(b_, h_, n, CHUNK, dv)
    b = jnp.cumsum(log_f.reshape(b_, h_, n, CHUNK, dk), axis=3)
    b_last = b[:, :, :, -1]

    k_to_end = kc * jnp.exp(b_last[:, :, :, None] - b)
    u_chunk = jnp.einsum('bhnck,bhncv->bhnkv', k_to_end, vc)
    decay = jnp.exp(b_last)

    def step(state, inp):
        d, u = inp
        return d[..., None] * state + u, state

    s0 = jnp.zeros((b_, h_, dk, dv), q.dtype)
    _, s_prev = lax.scan(step, s0, (jnp.moveaxis(decay, 2, 0), jnp.moveaxis(u_chunk, 2, 0)))
    s_prev = jnp.moveaxis(s_prev, 0, 2)
    o_state = jnp.einsum('bhnck,bhnkv->bhncv', qc * jnp.exp(b), s_prev)

    bs = b.reshape(b_, h_, n, ns, SUB_CHUNK, dk)
    qs = qc.reshape(b_, h_, n, ns, SUB_CHUNK, dk)
    ks = kc.reshape(b_, h_, n, ns, SUB_CHUNK, dk)
    vs = vc.reshape(b_, h_, n, ns, SUB_CHUNK, dv)
    b_ref = jnp.concatenate([jnp.zeros_like(bs[:, :, :, :1, -1]), bs[:, :, :, :-1, -1]], axis=3)
    q_ref = qs * jnp.exp(bs - b_ref[..., None, :])
    sub_id = jnp.arange(CHUNK) // SUB_CHUNK
    earlier = sub_id[None, :] < jnp.arange(ns)[:, None]
    k_exp = jnp.where(earlier[:, :, None],
                      b_ref[:, :, :, :, None, :] - b[:, :, :, None, :, :], -jnp.inf)
    k_ref = kc[:, :, :, None] * jnp.exp(k_exp)
    a_cross = jnp.einsum('bhnitk,bhnisk->bhnits', q_ref, k_ref)
    o_cross = jnp.einsum('bhnits,bhnsv->bhnitv', a_cross, vc)

    tril = jnp.tril(jnp.ones((SUB_CHUNK, SUB_CHUNK), bool))
    rel = jnp.where(tril[:, :, None], bs[..., :, None, :] - bs[..., None, :, :], -jnp.inf)
    a_diag = jnp.sum(qs[..., :, None, :] * ks[..., None, :, :] * jnp.exp(rel), axis=-1)
    o_diag = jnp.einsum('bhnits,bhnisv->bhnitv', a_diag, vs)

    o = o_state + (o_cross + o_diag).reshape(b_, h_, n, CHUNK, dv)
    return o.reshape(b_, h_, s_, dv)


def hybrid_mixer(h, w_in, pool_w, pool_scale, lb, hgrn_norm, w_pool_proj, w_hgrn_proj, w_out):
    b_, s_, _ = h.shape
    z = h @ w_in
    splits = [POOL_WIDTH, POOL_WIDTH + HGRN_FORGET, POOL_WIDTH + 2 * HGRN_FORGET,
              POOL_WIDTH + 2 * HGRN_FORGET + HGRN_INPUT, POOL_WIDTH + 2 * HGRN_FORGET + 2 * HGRN_INPUT]
    u_pool, q_pre, f_pre, i_in, og, gates = jnp.split(z, splits, axis=-1)

    pool_out = causal_pool_mixer(u_pool, pool_w, pool_scale)

    def heads(t, d):
        return t.reshape(b_, s_, HGRN_HEADS, d).transpose(0, 2, 1, 3).astype(jnp.float32)

    fz = heads(f_pre, HGRN_EXPAND)
    lbh = lb.astype(jnp.float32).reshape(HGRN_HEADS, 1, HGRN_EXPAND)
    log_f = jnp.logaddexp(jnp.log(lbh), jnp.log1p(-lbh) + jax.nn.log_sigmoid(fz))
    k = (1.0 - lbh) * jax.nn.sigmoid(-fz)
    q = jax.nn.silu(heads(q_pre, HGRN_EXPAND))
    v = heads(i_in, HGRN_HEAD_DIM)
    o = hgrn2_chunkwise(q, k, v, log_f)
    o = o * lax.rsqrt(jnp.mean(o * o, axis=-1, keepdims=True) + EPS)
    o = o.transpose(0, 2, 1, 3).reshape(b_, s_, HGRN_INPUT) * hgrn_norm.astype(jnp.float32)
    hgrn_out = (o * jax.nn.silu(og.astype(jnp.float32))).astype(h.dtype)

    g = jax.nn.sigmoid(gates).reshape(b_, s_, N_BRANCHES, D_MODEL)
    merged = g[:, :, 0] * (pool_out @ w_pool_proj) + g[:, :, 1] * (hgrn_out @ w_hgrn_proj)
    return merged @ w_out


def setup_inputs(seed: int = 0) -> dict:
    key = jax.random.key(seed)
    ks = jax.random.split(key, 20)

    def nrm(k, shape, fan_in):
        return jax.random.normal(k, shape, jnp.float32) * fan_in ** -0.5

    def gain(k, shape):
        return 1.0 + 0.05 * jax.random.normal(k, shape, jnp.float32)

    L, D = DEPTH, D_MODEL
    return {
        'x': jax.random.normal(ks[0], (BATCH, SEQ, D), jnp.float32),
        'ffn1_norm': gain(ks[1], (L, D)),
        'ffn1_w_gate': nrm(ks[2], (L, D, D_FF), D),
        'ffn1_w_up': nrm(ks[3], (L, D, D_FF), D),
        'ffn1_w_down': nrm(ks[4], (L, D_FF, D), D_FF),
        'mix_norm': gain(ks[5], (L, D)),
        'w_in': nrm(ks[6], (L, D, D_IN), D),
        'pool_w': nrm(ks[7], (L, POOL_GROUPS, POOL_GROUP_DIM, POOL_GROUP_DIM), POOL_GROUP_DIM),
        'pool_scale': gain(ks[8], (L, POOL_WIDTH)),
        'lb_logits': 0.5 * jax.random.normal(ks[9], (L, HGRN_FORGET), jnp.float32),
        'hgrn_norm': gain(ks[10], (L, HGRN_INPUT)),
        'w_pool_proj': nrm(ks[11], (L, POOL_WIDTH, D), POOL_WIDTH),
        'w_hgrn_proj': nrm(ks[12], (L, HGRN_INPUT, D), HGRN_INPUT),
        'w_out': nrm(ks[13], (L, D, D), D),
        'ffn2_norm': gain(ks[14], (L, D)),
        'ffn2_w_gate': nrm(ks[15], (L, D, D_FF), D),
        'ffn2_w_up': nrm(ks[16], (L, D, D_FF), D),
        'ffn2_w_down': nrm(ks[17], (L, D_FF, D), D_FF),
        'final_norm': gain(ks[18], (D,)),
    }


def reference(x, ffn1_norm, ffn1_w_gate, ffn1_w_up, ffn1_w_down, mix_norm, w_in, pool_w,
              pool_scale, lb_logits, hgrn_norm, w_pool_proj, w_hgrn_proj, w_out,
              ffn2_norm, ffn2_w_gate, ffn2_w_up, ffn2_w_down, final_norm):
    lb_all = jnp.cumsum(jax.nn.softmax(lb_logits.astype(jnp.float32), axis=0), axis=0)
    lb_all = lb_all - lb_all[:1]
    for l in range(DEPTH):
        x = x + 0.5 * swiglu(rmsnorm(x, ffn1_norm[l]), ffn1_w_gate[l], ffn1_w_up[l], ffn1_w_down[l])
        h = rmsnorm(x, mix_norm[l])
        x = x + hybrid_mixer(h, w_in[l], pool_w[l], pool_scale[l], lb_all[l], hgrn_norm[l],
                             w_pool_proj[l], w_hgrn_proj[l], w_out[l])
        x = x + 0.5 * swiglu(rmsnorm(x, ffn2_norm[l]), ffn2_w_gate[l], ffn2_w_up[l], ffn2_w_down[l])
    return rmsnorm(x, final_norm)
```

```python
import functools

import jax
import jax.numpy as jnp
from jax import lax
from jax.experimental import pallas as pl
from jax.experimental.pallas import tpu as pltpu

D_MODEL = 1024
D_FF = 2816
CHUNK = 64
SUB = 16
N_SUB = CHUNK // SUB
POOL_WIDTH = 512
POOL_WINDOWS = (2, 4, 8, 16)
POOL_GROUP_DIM = 128
POOL_CARRY = 16
HEADS = 4
HEAD_DIM = 128
HGRN_W = HEADS * HEAD_DIM
EPS = 1e-6

FF_TILE = 256
N_FF_TILES = D_FF // FF_TILE
TM_FFN = 512
TM_MIX = 512
VMEM_LIMIT_BYTES = 56 * 1024 * 1024

OFF_POOL = 0
OFF_Q = POOL_WIDTH
OFF_F = OFF_Q + HGRN_W
OFF_I = OFF_F + HGRN_W
OFF_OG = OFF_I + HGRN_W
OFF_GA = OFF_OG + HGRN_W
OFF_GB = OFF_GA + D_MODEL
D_IN = OFF_GB + D_MODEL

F32 = jnp.float32
BF16 = jnp.bfloat16


def _dot(a, b):
    return jnp.dot(a, b, preferred_element_type=F32)


def _dot_nt(a, b):
    return lax.dot_general(a, b, (((1,), (1,)), ((), ())), preferred_element_type=F32)


def _dot_tn(a, b):
    return lax.dot_general(a, b, (((0,), (0,)), ((), ())), preferred_element_type=F32)


def _rmsnorm(x, g):
    ms = jnp.mean(x * x, axis=-1, keepdims=True)
    return x * lax.rsqrt(ms + EPS) * g


def _sigmoid(x):
    return 1.0 / (1.0 + jnp.exp(-x))


def _ffn_kernel(x_ref, g_ref, wg_ref, wu_ref, wd_ref, fg_ref, o_ref, h_ref, acc_ref, *, final_norm):
    x = x_ref[...]
    h_ref[...] = _rmsnorm(x, g_ref[...]).astype(BF16)
    acc_ref[...] = jnp.zeros_like(acc_ref)

    def tile(j, carry):
        h = h_ref[...]
        g = _dot(h, wg_ref[j])
        u = _dot(h, wu_ref[j])
        a = (g * _sigmoid(g) * u).astype(BF16)
        acc_ref[...] += _dot(a, wd_ref[j])
        return carry

    lax.fori_loop(0, N_FF_TILES, tile, 0)
    y = x + 0.5 * acc_ref[...]
    if final_norm:
        y = _rmsnorm(y, fg_ref[...])
    o_ref[...] = y


def _const_spec(shape):
    nd = len(shape)
    return pl.BlockSpec(shape, lambda *_: (0,) * nd, pipeline_mode=pl.Buffered(1))


def _ffn(x, g, wg, wu, wd, fg, final_norm):
    n = x.shape[0]
    return pl.pallas_call(
        functools.partial(_ffn_kernel, final_norm=final_norm),
        grid=(n // TM_FFN,),
        in_specs=[
            pl.BlockSpec((TM_FFN, D_MODEL), lambda i: (i, 0)),
            _const_spec((1, D_MODEL)),
            _const_spec((N_FF_TILES, D_MODEL, FF_TILE)),
            _const_spec((N_FF_TILES, D_MODEL, FF_TILE)),
            _const_spec((N_FF_TILES, FF_TILE, D_MODEL)),
            _const_spec((1, D_MODEL)),
        ],
        out_specs=pl.BlockSpec((TM_FFN, D_MODEL), lambda i: (i, 0)),
        out_shape=jax.ShapeDtypeStruct((n, D_MODEL), F32),
        scratch_shapes=[
            pltpu.VMEM((TM_FFN, D_MODEL), BF16),
            pltpu.VMEM((TM_FFN, D_MODEL), F32),
        ],
        compiler_params=pltpu.CompilerParams(
            dimension_semantics=("arbitrary",), vmem_limit_bytes=VMEM_LIMIT_BYTES),
        name="swiglu_half_step",
    )(x, g, wg, wu, wd, fg)


def _lower_bound(lbl_ref, layer):
    logits = lbl_ref[...]
    m = jnp.max(logits, axis=0, keepdims=True)
    e = jnp.exp(logits - m)
    p = e / jnp.sum(e, axis=0, keepdims=True)
    lb = jnp.zeros((1, HGRN_W), F32)
    for j in range(1, layer + 1):
        lb = lb + p[j:j + 1, :]
    return lb


def _hgrn_chunk(ci, zq_ref, zf_ref, zi_ref, oh_ref, st_ref, log_lb, log1m_lb, one_m_lb):
    r0 = pl.multiple_of(ci * CHUNK, CHUNK)
    rows = pl.ds(r0, CHUNK)
    fz = zf_ref[rows, :]
    qz = zq_ref[rows, :]
    v = zi_ref[rows, :]

    e = jnp.exp(-jnp.abs(fz))
    r = 1.0 / (1.0 + e)
    sig_neg = jnp.where(fz >= 0, e * r, r)
    log_sig = jnp.minimum(fz, 0.0) - jnp.log1p(e)
    bb = log1m_lb + log_sig
    log_f = jnp.maximum(log_lb, bb) + jnp.log1p(jnp.exp(-jnp.abs(log_lb - bb)))
    k = one_m_lb * sig_neg
    q = qz * _sigmoid(qz)

    row_in_sub = lax.broadcasted_iota(jnp.int32, (CHUNK, HGRN_W), 0) & (SUB - 1)
    c = log_f
    for d in (1, 2, 4, 8):
        c = c + jnp.where(row_in_sub >= d, pltpu.roll(c, d, axis=0), 0.0)

    tot = [c[SUB * j + SUB - 1:SUB * j + SUB, :] for j in range(N_SUB)]
    tot_own = jnp.concatenate([jnp.broadcast_to(t, (SUB, HGRN_W)) for t in tot], axis=0)
    qh = q * jnp.exp(c)
    kh = k * jnp.exp(tot_own - c)

    def sub(a, j):
        return a[SUB * j:SUB * (j + 1), :]

    e_t = [None] + [jnp.exp(tot[j]) for j in range(1, N_SUB)]
    p1 = tot[0]
    p2 = p1 + tot[1]
    p3 = p2 + tot[2]
    p4 = p3 + tot[3]
    qg = jnp.concatenate([sub(qh, 0), sub(qh, 1) * jnp.exp(p1), sub(qh, 2) * jnp.exp(p2),
                          sub(qh, 3) * jnp.exp(p3)], axis=0)
    decay = jnp.exp(p4)

    zero = jnp.zeros((SUB, HGRN_W), F32)
    k2_0 = sub(kh, 0) * e_t[1]
    k3_0 = k2_0 * e_t[2]
    k3_1 = sub(kh, 1) * e_t[2]
    k1 = jnp.concatenate([sub(kh, 0), zero, zero, zero], axis=0).astype(BF16)
    k2 = jnp.concatenate([k2_0, sub(kh, 1), zero, zero], axis=0).astype(BF16)
    k3 = jnp.concatenate([k3_0, k3_1, sub(kh, 2), zero], axis=0).astype(BF16)
    kg = jnp.concatenate([k3_0 * e_t[3], k3_1 * e_t[3], sub(kh, 2) * e_t[3], sub(kh, 3)],
                         axis=0).astype(BF16)
    qh_b = qh.astype(BF16)
    qg_b = qg.astype(BF16)
    v_b = v.astype(BF16)

    row16 = lax.broadcasted_iota(jnp.int32, (SUB, HGRN_W), 0)
    o_diag = []
    for j in range(N_SUB):
        cj, qj, kj, vj = sub(c, j), sub(q, j), sub(k, j), sub(v, j)
        acc = jnp.zeros((SUB, HGRN_W), F32)
        for s in range(SUB):
            pe = jnp.where(row16 >= s, jnp.exp(cj - cj[s:s + 1, :]), 0.0)
            pm = qj * kj[s:s + 1, :] * pe
            cols = []
            for hd in range(HEADS):
                hs = slice(hd * HEAD_DIM, (hd + 1) * HEAD_DIM)
                col = jnp.sum(pm[:, hs], axis=-1, keepdims=True)
                cols.append(col * vj[s:s + 1, hs])
            acc = acc + jnp.concatenate(cols, axis=-1)
        o_diag.append(acc)
    o_diag = jnp.concatenate(o_diag, axis=0)

    outs = []
    zero_a = jnp.zeros((SUB, CHUNK), F32)
    for hd in range(HEADS):
        hs = slice(hd * HEAD_DIM, (hd + 1) * HEAD_DIM)
        st = st_ref[hd]
        o_state = _dot_nt(qg_b[:, hs], st.astype(BF16))
        a1 = _dot_nt(qh_b[SUB:2 * SUB, hs], k1[:, hs])
        a2 = _dot_nt(qh_b[2 * SUB:3 * SUB, hs], k2[:, hs])
        a3 = _dot_nt(qh_b[3 * SUB:4 * SUB, hs], k3[:, hs])
        a = jnp.concatenate([zero_a, a1, a2, a3], axis=0).astype(BF16)
        o_cross = _dot(a, v_b[:, hs])
        st_ref[hd] = decay[:, hs] * st + _dot_tn(v_b[:, hs], kg[:, hs])
        outs.append(o_state + o_cross)
    oh_ref[rows, :] = jnp.concatenate(outs, axis=-1) + o_diag


def _mixer_kernel(x_ref, g_ref, win_ref, poolw_ref, pscale_ref, lbl_ref, hnorm_ref, wpp_ref, whp_ref,
                  wout_ref, o_ref, h_ref, zq_ref, zf_ref, zi_ref, ext_ref, oh_ref, st_ref, *, layer):
    tb = pl.program_id(1)

    @pl.when(tb == 0)
    def _():
        st_ref[...] = jnp.zeros_like(st_ref)
        ext_ref[0:POOL_CARRY, :] = jnp.zeros((POOL_CARRY, POOL_WIDTH), F32)

    x = x_ref[...]
    h_ref[...] = _rmsnorm(x, g_ref[...]).astype(BF16)
    h = h_ref[...]

    zq_ref[...] = _dot(h, win_ref[:, OFF_Q:OFF_Q + HGRN_W])
    zf_ref[...] = _dot(h, win_ref[:, OFF_F:OFF_F + HGRN_W])
    zi_ref[...] = _dot(h, win_ref[:, OFF_I:OFF_I + HGRN_W])

    lb = _lower_bound(lbl_ref, layer)
    log_lb = jnp.log(lb)
    log1m_lb = jnp.log1p(-lb)
    one_m_lb = 1.0 - lb

    def chunk(ci, carry):
        _hgrn_chunk(ci, zq_ref, zf_ref, zi_ref, oh_ref, st_ref, log_lb, log1m_lb, one_m_lb)
        return carry

    lax.fori_loop(0, TM_MIX // CHUNK, chunk, 0)

    o = oh_ref[...]
    normed = []
    for hd in range(HEADS):
        oh = o[:, hd * HEAD_DIM:(hd + 1) * HEAD_DIM]
        ms = jnp.mean(oh * oh, axis=-1, keepdims=True)
        normed.append(oh * lax.rsqrt(ms + EPS))
    on = jnp.concatenate(normed, axis=-1) * hnorm_ref[...]
    og = _dot(h, win_ref[:, OFF_OG:OFF_OG + HGRN_W])
    hgrn_out = (on * (og * _sigmoid(og))).astype(BF16)
    pb = _dot(hgrn_out, whp_ref[...])

    ext_ref[POOL_CARRY:POOL_CARRY + TM_MIX, :] = _dot(h, win_ref[:, OFF_POOL:OFF_POOL + POOL_WIDTH])
    pos = lax.broadcasted_iota(jnp.int32, (TM_MIX, POOL_GROUP_DIM), 0) + (tb * TM_MIX + 1)
    mixed = []
    for gi, w in enumerate(POOL_WINDOWS):
        cols = slice(gi * POOL_GROUP_DIM, (gi + 1) * POOL_GROUP_DIM)
        u = ext_ref[POOL_CARRY:POOL_CARRY + TM_MIX, cols]
        wsum = u
        for j in range(1, w):
            wsum = wsum + ext_ref[POOL_CARRY - j:POOL_CARRY - j + TM_MIX, cols]
        count = jnp.minimum(pos, w).astype(F32)
        delta = (wsum / count - u).astype(BF16)
        mixed.append(_dot(delta, poolw_ref[gi]))
    pool_out = (jnp.concatenate(mixed, axis=-1) * pscale_ref[...]).astype(BF16)
    ext_ref[0:POOL_CARRY, :] = ext_ref[TM_MIX:TM_MIX + POOL_CARRY, :]
    pa = _dot(pool_out, wpp_ref[...])

    ga = _dot(h, win_ref[:, OFF_GA:OFF_GA + D_MODEL])
    gb = _dot(h, win_ref[:, OFF_GB:OFF_GB + D_MODEL])
    merged = (_sigmoid(ga) * pa + _sigmoid(gb) * pb).astype(BF16)
    o_ref[...] = x + _dot(merged, wout_ref[...])


def _mixer(x, g, win, poolw, pscale, lbl, hnorm, wpp, whp, wout, layer, batch):
    n = x.shape[0]
    nt = n // batch // TM_MIX
    depth = lbl.shape[0]
    return pl.pallas_call(
        functools.partial(_mixer_kernel, layer=layer),
        grid=(batch, nt),
        in_specs=[
            pl.BlockSpec((TM_MIX, D_MODEL), lambda b, t: (b * nt + t, 0)),
            _const_spec((1, D_MODEL)),
            _const_spec((D_MODEL, D_IN)),
            _const_spec((len(POOL_WINDOWS), POOL_GROUP_DIM, POOL_GROUP_DIM)),
            _const_spec((1, POOL_WIDTH)),
            _const_spec((depth, HGRN_W)),
            _const_spec((1, HGRN_W)),
            _const_spec((POOL_WIDTH, D_MODEL)),
            _const_spec((HGRN_W, D_MODEL)),
            _const_spec((D_MODEL, D_MODEL)),
        ],
        out_specs=pl.BlockSpec((TM_MIX, D_MODEL), lambda b, t: (b * nt + t, 0)),
        out_shape=jax.ShapeDtypeStruct((n, D_MODEL), F32),
        scratch_shapes=[
            pltpu.VMEM((TM_MIX, D_MODEL), BF16),
            pltpu.VMEM((TM_MIX, HGRN_W), F32),
            pltpu.VMEM((TM_MIX, HGRN_W), F32),
            pltpu.VMEM((TM_MIX, HGRN_W), F32),
            pltpu.VMEM((POOL_CARRY + TM_MIX, POOL_WIDTH), F32),
            pltpu.VMEM((TM_MIX, HGRN_W), F32),
            pltpu.VMEM((HEADS, HEAD_DIM, HEAD_DIM), F32),
        ],
        compiler_params=pltpu.CompilerParams(
            dimension_semantics=("arbitrary", "arbitrary"), vmem_limit_bytes=VMEM_LIMIT_BYTES),
        name="hybrid_mixer",
    )(x, g, win, poolw, pscale, lbl, hnorm, wpp, whp, wout)


def _ff_in_tiles(w):
    return w.astype(BF16).reshape(D_MODEL, N_FF_TILES, FF_TILE).transpose(1, 0, 2)


def _ff_out_tiles(w):
    return w.astype(BF16).reshape(N_FF_TILES, FF_TILE, D_MODEL)


def kernel(x, ffn1_norm, ffn1_w_gate, ffn1_w_up, ffn1_w_down, mix_norm, w_in, pool_w, pool_scale,
           lb_logits, hgrn_norm, w_pool_proj, w_hgrn_proj, w_out, ffn2_norm, ffn2_w_gate, ffn2_w_up,
           ffn2_w_down, final_norm):
    batch, seq, _ = x.shape
    depth = ffn1_norm.shape[0]
    xf = x.reshape(batch * seq, D_MODEL)
    fg = final_norm.reshape(1, D_MODEL)
    for l in range(depth):
        xf = _ffn(xf, ffn1_norm[l].reshape(1, D_MODEL), _ff_in_tiles(ffn1_w_gate[l]),
                  _ff_in_tiles(ffn1_w_up[l]), _ff_out_tiles(ffn1_w_down[l]), fg, False)
        xf = _mixer(xf, mix_norm[l].reshape(1, D_MODEL), w_in[l].astype(BF16), pool_w[l].astype(BF16),
                    pool_scale[l].reshape(1, POOL_WIDTH), lb_logits, hgrn_norm[l].reshape(1, HGRN_W),
                    w_pool_proj[l].astype(BF16), w_hgrn_proj[l].astype(BF16), w_out[l].astype(BF16),
                    l, batch)
        xf = _ffn(xf, ffn2_norm[l].reshape(1, D_MODEL), _ff_in_tiles(ffn2_w_gate[l]),
                  _ff_in_tiles(ffn2_w_up[l]), _ff_out_tiles(ffn2_w_down[l]), fg, l == depth - 1)
    return xf.reshape(batch, seq, D_MODEL)
```

```python
import functools

import jax
import jax.numpy as jnp
from jax import lax
from jax.experimental import pallas as pl
from jax.experimental.pallas import tpu as pltpu

D_MODEL = 1024
D_FF = 2816
CHUNK = 64
SUB = 16
N_SUB = CHUNK // SUB
POOL_WIDTH = 512
POOL_WINDOWS = (2, 4, 8, 16)
POOL_GROUP_DIM = 128
POOL_CARRY = 16
HEADS = 4
HEAD_DIM = 128
HGRN_W = HEADS * HEAD_DIM
EPS = 1e-6

FF_TILE = 256
N_FF_TILES = D_FF // FF_TILE
TM_FFN = 512
TM_MIX = 512
VMEM_LIMIT_BYTES = 56 * 1024 * 1024

OFF_POOL = 0
OFF_Q = POOL_WIDTH
OFF_F = OFF_Q + HGRN_W
OFF_I = OFF_F + HGRN_W
OFF_OG = OFF_I + HGRN_W
OFF_GA = OFF_OG + HGRN_W
OFF_GB = OFF_GA + D_MODEL
D_IN = OFF_GB + D_MODEL

F32 = jnp.float32
BF16 = jnp.bfloat16


def _dot(a, b):
    return jnp.dot(a, b, preferred_element_type=F32)


def _dot_nt(a, b):
    return lax.dot_general(a, b, (((1,), (1,)), ((), ())), preferred_element_type=F32)


def _dot_tn(a, b):
    return lax.dot_general(a, b, (((0,), (0,)), ((), ())), preferred_element_type=F32)


def _rmsnorm(x, g):
    ms = jnp.mean(x * x, axis=-1, keepdims=True)
    return x * lax.rsqrt(ms + EPS) * g


def _sigmoid(x):
    return 1.0 / (1.0 + jnp.exp(-x))


def _ffn_kernel(x_ref, g_ref, wg_ref, wu_ref, wd_ref, fg_ref, o_ref, h_ref, a_ref, *, final_norm):
    x = x_ref[...]
    h_ref[...] = _rmsnorm(x, g_ref[...]).astype(BF16)
    for j in range(N_FF_TILES):
        cols = slice(j * FF_TILE, (j + 1) * FF_TILE)
        h = h_ref[...]
        g = _dot(h, wg_ref[:, cols])
        u = _dot(h, wu_ref[:, cols])
        a_ref[:, cols] = (g * _sigmoid(g) * u).astype(BF16)
    y = x + 0.5 * _dot(a_ref[...], wd_ref[...])
    if final_norm:
        y = _rmsnorm(y, fg_ref[...])
    o_ref[...] = y


def _const_spec(shape):
    nd = len(shape)
    return pl.BlockSpec(shape, lambda *_: (0,) * nd, pipeline_mode=pl.Buffered(1))


def _ffn(x, g, wg, wu, wd, fg, final_norm):
    n = x.shape[0]
    return pl.pallas_call(
        functools.partial(_ffn_kernel, final_norm=final_norm),
        grid=(n // TM_FFN,),
        in_specs=[
            pl.BlockSpec((TM_FFN, D_MODEL), lambda i: (i, 0)),
            _const_spec((1, D_MODEL)),
            _const_spec((D_MODEL, D_FF)),
            _const_spec((D_MODEL, D_FF)),
            _const_spec((D_FF, D_MODEL)),
            _const_spec((1, D_MODEL)),
        ],
        out_specs=pl.BlockSpec((TM_FFN, D_MODEL), lambda i: (i, 0)),
        out_shape=jax.ShapeDtypeStruct((n, D_MODEL), F32),
        scratch_shapes=[
            pltpu.VMEM((TM_FFN, D_MODEL), BF16),
            pltpu.VMEM((TM_FFN, D_FF), BF16),
        ],
        compiler_params=pltpu.CompilerParams(
            dimension_semantics=("arbitrary",), vmem_limit_bytes=VMEM_LIMIT_BYTES),
        name="swiglu_half_step",
    )(x, g, wg, wu, wd, fg)


def _lower_bound(lbl_ref, layer):
    logits = lbl_ref[...]
    m = jnp.max(logits, axis=0, keepdims=True)
    e = jnp.exp(logits - m)
    p = e / jnp.sum(e, axis=0, keepdims=True)
    lb = jnp.zeros((1, HGRN_W), F32)
    for j in range(1, layer + 1):
        lb = lb + p[j:j + 1, :]
    return lb


def _hgrn_chunk(ci, zq_ref, zf_ref, zi_ref, oh_ref, st_ref, log_lb, log1m_lb, one_m_lb):
    r0 = pl.multiple_of(ci * CHUNK, CHUNK)
    rows = pl.ds(r0, CHUNK)
    fz = zf_ref[rows, :]
    qz = zq_ref[rows, :]
    v = zi_ref[rows, :]

    e = jnp.exp(-jnp.abs(fz))
    r = 1.0 / (1.0 + e)
    sig_neg = jnp.where(fz >= 0, e * r, r)
    log_sig = jnp.minimum(fz, 0.0) - jnp.log1p(e)
    bb = log1m_lb + log_sig
    log_f = jnp.maximum(log_lb, bb) + jnp.log1p(jnp.exp(-jnp.abs(log_lb - bb)))
    k = one_m_lb * sig_neg
    q = qz * _sigmoid(qz)

    row_in_sub = lax.broadcasted_iota(jnp.int32, (CHUNK, HGRN_W), 0) & (SUB - 1)
    c = log_f
    for d in (1, 2, 4, 8):
        c = c + jnp.where(row_in_sub >= d, pltpu.roll(c, d, axis=0), 0.0)

    tot = [c[SUB * j + SUB - 1:SUB * j + SUB, :] for j in range(N_SUB)]
    tot_own = jnp.concatenate([jnp.broadcast_to(t, (SUB, HGRN_W)) for t in tot], axis=0)
    qh = q * jnp.exp(c)
    kh = k * jnp.exp(tot_own - c)

    def sub(a, j):
        return a[SUB * j:SUB * (j + 1), :]

    e_t = [None] + [jnp.exp(tot[j]) for j in range(1, N_SUB)]
    p1 = tot[0]
    p2 = p1 + tot[1]
    p3 = p2 + tot[2]
    p4 = p3 + tot[3]
    qg = jnp.concatenate([sub(qh, 0), sub(qh, 1) * jnp.exp(p1), sub(qh, 2) * jnp.exp(p2),
                          sub(qh, 3) * jnp.exp(p3)], axis=0)
    decay = jnp.exp(p4)

    zero = jnp.zeros((SUB, HGRN_W), F32)
    k2_0 = sub(kh, 0) * e_t[1]
    k3_0 = k2_0 * e_t[2]
    k3_1 = sub(kh, 1) * e_t[2]
    k1 = jnp.concatenate([sub(kh, 0), zero, zero, zero], axis=0).astype(BF16)
    k2 = jnp.concatenate([k2_0, sub(kh, 1), zero, zero], axis=0).astype(BF16)
    k3 = jnp.concatenate([k3_0, k3_1, sub(kh, 2), zero], axis=0).astype(BF16)
    kg = jnp.concatenate([k3_0 * e_t[3], k3_1 * e_t[3], sub(kh, 2) * e_t[3], sub(kh, 3)],
                         axis=0).astype(BF16)
    qh_b = qh.astype(BF16)
    qg_b = qg.astype(BF16)
    v_b = v.astype(BF16)

    row16 = lax.broadcasted_iota(jnp.int32, (SUB, HGRN_W), 0)
    o_diag = []
    for j in range(N_SUB):
        cj, qj, kj, vj = sub(c, j), sub(q, j), sub(k, j), sub(v, j)
        acc = jnp.zeros((SUB, HGRN_W), F32)
        for s in range(SUB):
            pe = jnp.where(row16 >= s, jnp.exp(cj - cj[s:s + 1, :]), 0.0)
            pm = qj * kj[s:s + 1, :] * pe
            cols = []
            for hd in range(HEADS):
                hs = slice(hd * HEAD_DIM, (hd + 1) * HEAD_DIM)
                col = jnp.sum(pm[:, hs], axis=-1, keepdims=True)
                cols.append(col * vj[s:s + 1, hs])
            acc = acc + jnp.concatenate(cols, axis=-1)
        o_diag.append(acc)
    o_diag = jnp.concatenate(o_diag, axis=0)

    outs = []
    zero_a = jnp.zeros((SUB, CHUNK), F32)
    for hd in range(HEADS):
        hs = slice(hd * HEAD_DIM, (hd + 1) * HEAD_DIM)
        st = st_ref[hd]
        o_state = _dot_nt(qg_b[:, hs], st.astype(BF16))
        a1 = _dot_nt(qh_b[SUB:2 * SUB, hs], k1[:, hs])
        a2 = _dot_nt(qh_b[2 * SUB:3 * SUB, hs], k2[:, hs])
        a3 = _dot_nt(qh_b[3 * SUB:4 * SUB, hs], k3[:, hs])
        a = jnp.concatenate([zero_a, a1, a2, a3], axis=0).astype(BF16)
        o_cross = _dot(a, v_b[:, hs])
        st_ref[hd] = decay[:, hs] * st + _dot_tn(v_b[:, hs], kg[:, hs])
        outs.append(o_state + o_cross)
    oh_ref[rows, :] = jnp.concatenate(outs, axis=-1) + o_diag


def _mixer_kernel(x_ref, g_ref, win_ref, poolw_ref, pscale_ref, lbl_ref, hnorm_ref, wpp_ref, whp_ref,
                  wout_ref, o_ref, h_ref, zq_ref, zf_ref, zi_ref, ext_ref, oh_ref, st_ref, *, layer):
    tb = pl.program_id(1)

    @pl.when(tb == 0)
    def _():
        st_ref[...] = jnp.zeros_like(st_ref)
        ext_ref[0:POOL_CARRY, :] = jnp.zeros((POOL_CARRY, POOL_WIDTH), F32)

    x = x_ref[...]
    h_ref[...] = _rmsnorm(x, g_ref[...]).astype(BF16)
    h = h_ref[...]

    zq_ref[...] = _dot(h, win_ref[:, OFF_Q:OFF_Q + HGRN_W])
    zf_ref[...] = _dot(h, win_ref[:, OFF_F:OFF_F + HGRN_W])
    zi_ref[...] = _dot(h, win_ref[:, OFF_I:OFF_I + HGRN_W])

    lb = _lower_bound(lbl_ref, layer)
    log_lb = jnp.log(lb)
    log1m_lb = jnp.log1p(-lb)
    one_m_lb = 1.0 - lb

    def chunk(ci, carry):
        _hgrn_chunk(ci, zq_ref, zf_ref, zi_ref, oh_ref, st_ref, log_lb, log1m_lb, one_m_lb)
        return carry

    lax.fori_loop(0, TM_MIX // CHUNK, chunk, 0)

    o = oh_ref[...]
    normed = []
    for hd in range(HEADS):
        oh = o[:, hd * HEAD_DIM:(hd + 1) * HEAD_DIM]
        ms = jnp.mean(oh * oh, axis=-1, keepdims=True)
        normed.append(oh * lax.rsqrt(ms + EPS))
    on = jnp.concatenate(normed, axis=-1) * hnorm_ref[...]
    og = _dot(h, win_ref[:, OFF_OG:OFF_OG + HGRN_W])
    hgrn_out = (on * (og * _sigmoid(og))).astype(BF16)
    pb = _dot(hgrn_out, whp_ref[...])

    ext_ref[POOL_CARRY:POOL_CARRY + TM_MIX, :] = _dot(h, win_ref[:, OFF_POOL:OFF_POOL + POOL_WIDTH])
    pos = lax.broadcasted_iota(jnp.int32, (TM_MIX, POOL_GROUP_DIM), 0) + (tb * TM_MIX + 1)
    mixed = []
    for gi, w in enumerate(POOL_WINDOWS):
        cols = slice(gi * POOL_GROUP_DIM, (gi + 1) * POOL_GROUP_DIM)
        u = ext_ref[POOL_CARRY:POOL_CARRY + TM_MIX, cols]
        wsum = u
        for j in range(1, w):
            wsum = wsum + ext_ref[POOL_CARRY - j:POOL_CARRY - j + TM_MIX, cols]
        count = jnp.minimum(pos, w).astype(F32)
        delta = (wsum / count - u).astype(BF16)
        mixed.append(_dot(delta, poolw_ref[gi]))
    pool_out = (jnp.concatenate(mixed, axis=-1) * pscale_ref[...]).astype(BF16)
    ext_ref[0:POOL_CARRY, :] = ext_ref[TM_MIX:TM_MIX + POOL_CARRY, :]
    pa = _dot(pool_out, wpp_ref[...])

    ga = _dot(h, win_ref[:, OFF_GA:OFF_GA + D_MODEL])
    gb = _dot(h, win_ref[:, OFF_GB:OFF_GB + D_MODEL])
    merged = (_sigmoid(ga) * pa + _sigmoid(gb) * pb).astype(BF16)
    o_ref[...] = x + _dot(merged, wout_ref[...])


def _mixer(x, g, win, poolw, pscale, lbl, hnorm, wpp, whp, wout, layer, batch):
    n = x.shape[0]
    nt = n // batch // TM_MIX
    depth = lbl.shape[0]
    return pl.pallas_call(
        functools.partial(_mixer_kernel, layer=layer),
        grid=(batch, nt),
        in_specs=[
            pl.BlockSpec((TM_MIX, D_MODEL), lambda b, t: (b * nt + t, 0)),
            _const_spec((1, D_MODEL)),
            _const_spec((D_MODEL, D_IN)),
            _const_spec((len(POOL_WINDOWS), POOL_GROUP_DIM, POOL_GROUP_DIM)),
            _const_spec((1, POOL_WIDTH)),
            _const_spec((depth, HGRN_W)),
            _const_spec((1, HGRN_W)),
            _const_spec((POOL_WIDTH, D_MODEL)),
            _const_spec((HGRN_W, D_MODEL)),
            _const_spec((D_MODEL, D_MODEL)),
        ],
        out_specs=pl.BlockSpec((TM_MIX, D_MODEL), lambda b, t: (b * nt + t, 0)),
        out_shape=jax.ShapeDtypeStruct((n, D_MODEL), F32),
        scratch_shapes=[
            pltpu.VMEM((TM_MIX, D_MODEL), BF16),
            pltpu.VMEM((TM_MIX, HGRN_W), F32),
            pltpu.VMEM((TM_MIX, HGRN_W), F32),
            pltpu.VMEM((TM_MIX, HGRN_W), F32),
            pltpu.VMEM((POOL_CARRY + TM_MIX, POOL_WIDTH), F32),
            pltpu.VMEM((TM_MIX, HGRN_W), F32),
            pltpu.VMEM((HEADS, HEAD_DIM, HEAD_DIM), F32),
        ],
        compiler_params=pltpu.CompilerParams(
            dimension_semantics=("arbitrary", "arbitrary"), vmem_limit_bytes=VMEM_LIMIT_BYTES),
        name="hybrid_mixer",
    )(x, g, win, poolw, pscale, lbl, hnorm, wpp, whp, wout)


def kernel(x, ffn1_norm, ffn1_w_gate, ffn1_w_up, ffn1_w_down, mix_norm, w_in, pool_w, pool_scale,
           lb_logits, hgrn_norm, w_pool_proj, w_hgrn_proj, w_out, ffn2_norm, ffn2_w_gate, ffn2_w_up,
           ffn2_w_down, final_norm):
    batch, seq, _ = x.shape
    depth = ffn1_norm.shape[0]
    xf = x.reshape(batch * seq, D_MODEL)
    fg = final_norm.reshape(1, D_MODEL)
    for l in range(depth):
        xf = _ffn(xf, ffn1_norm[l].reshape(1, D_MODEL), ffn1_w_gate[l].astype(BF16),
                  ffn1_w_up[l].astype(BF16), ffn1_w_down[l].astype(BF16), fg, False)
        xf = _mixer(xf, mix_norm[l].reshape(1, D_MODEL), w_in[l].astype(BF16), pool_w[l].astype(BF16),
                    pool_scale[l].reshape(1, POOL_WIDTH), lb_logits, hgrn_norm[l].reshape(1, HGRN_W),
                    w_pool_proj[l].astype(BF16), w_hgrn_proj[l].astype(BF16), w_out[l].astype(BF16),
                    l, batch)
        xf = _ffn(xf, ffn2_norm[l].reshape(1, D_MODEL), ffn2_w_gate[l].astype(BF16),
                  ffn2_w_up[l].astype(BF16), ffn2_w_down[l].astype(BF16), fg, l == depth - 1)
    return xf.reshape(batch, seq, D_MODEL)
```

```python
import functools

import jax
import jax.numpy as jnp
from jax import lax
from jax.experimental import pallas as pl
from jax.experimental.pallas import tpu as pltpu

D_MODEL = 1024
D_FF = 2816
CHUNK = 64
SUB = 16
N_SUB = CHUNK // SUB
POOL_WIDTH = 512
POOL_WINDOWS = (2, 4, 8, 16)
POOL_GROUP_DIM = 128
POOL_CARRY = 16
HEADS = 4
HEAD_DIM = 128
HGRN_W = HEADS * HEAD_DIM
EPS = 1e-6
LOG2E = 1.4426950408889634

FF_TILE = 256
N_FF_TILES = D_FF // FF_TILE
TM_FFN = 512
TM_MIX = 512
VMEM_LIMIT_BYTES = 56 * 1024 * 1024

OFF_POOL = 0
OFF_Q = POOL_WIDTH
OFF_F = OFF_Q + HGRN_W
OFF_I = OFF_F + HGRN_W
OFF_OG = OFF_I + HGRN_W
OFF_GA = OFF_OG + HGRN_W
OFF_GB = OFF_GA + D_MODEL
D_IN = OFF_GB + D_MODEL

F32 = jnp.float32
BF16 = jnp.bfloat16


def _dot(a, b):
    return jnp.dot(a, b, preferred_element_type=F32)


def _dot_nt(a, b):
    return lax.dot_general(a, b, (((1,), (1,)), ((), ())), preferred_element_type=F32)


def _dot_tn(a, b):
    return lax.dot_general(a, b, (((0,), (0,)), ((), ())), preferred_element_type=F32)


def _rmsnorm(x, g):
    ms = jnp.mean(x * x, axis=-1, keepdims=True)
    return x * lax.rsqrt(ms + EPS) * g


def _sigmoid(x):
    return 1.0 / (1.0 + jnp.exp(-x))


def _ffn_kernel(x_ref, g_ref, wg_ref, wu_ref, wd_ref, fg_ref, o_ref, h_ref, a_ref, *, final_norm):
    x = x_ref[...]
    h_ref[...] = _rmsnorm(x, g_ref[...]).astype(BF16)
    for j in range(N_FF_TILES):
        cols = slice(j * FF_TILE, (j + 1) * FF_TILE)
        h = h_ref[...]
        g = _dot(h, wg_ref[:, cols])
        u = _dot(h, wu_ref[:, cols])
        a_ref[:, cols] = (g * _sigmoid(g) * u).astype(BF16)
    y = x + 0.5 * _dot(a_ref[...], wd_ref[...])
    if final_norm:
        y = _rmsnorm(y, fg_ref[...])
    o_ref[...] = y


def _const_spec(shape):
    nd = len(shape)
    return pl.BlockSpec(shape, lambda *_: (0,) * nd, pipeline_mode=pl.Buffered(1))


def _ffn(x, g, wg, wu, wd, fg, final_norm):
    n = x.shape[0]
    return pl.pallas_call(
        functools.partial(_ffn_kernel, final_norm=final_norm),
        grid=(n // TM_FFN,),
        in_specs=[
            pl.BlockSpec((TM_FFN, D_MODEL), lambda i: (i, 0)),
            _const_spec((1, D_MODEL)),
            _const_spec((D_MODEL, D_FF)),
            _const_spec((D_MODEL, D_FF)),
            _const_spec((D_FF, D_MODEL)),
            _const_spec((1, D_MODEL)),
        ],
        out_specs=pl.BlockSpec((TM_FFN, D_MODEL), lambda i: (i, 0)),
        out_shape=jax.ShapeDtypeStruct((n, D_MODEL), F32),
        scratch_shapes=[
            pltpu.VMEM((TM_FFN, D_MODEL), BF16),
            pltpu.VMEM((TM_FFN, D_FF), BF16),
        ],
        compiler_params=pltpu.CompilerParams(
            dimension_semantics=("arbitrary",), vmem_limit_bytes=VMEM_LIMIT_BYTES),
        name="swiglu_half_step",
    )(x, g, wg, wu, wd, fg)


def _lower_bound(lbl_ref, layer):
    logits = lbl_ref[...]
    m = jnp.max(logits, axis=0, keepdims=True)
    e = jnp.exp(logits - m)
    p = e / jnp.sum(e, axis=0, keepdims=True)
    lb = jnp.zeros((1, HGRN_W), F32)
    for j in range(1, layer + 1):
        lb = lb + p[j:j + 1, :]
    return lb


def _head(hd):
    return slice(hd * HEAD_DIM, (hd + 1) * HEAD_DIM)


def _sub_scan(lf2):
    n8 = CHUNK // 8
    w = lf2.reshape(n8, 8, HGRN_W)
    sub8 = lax.broadcasted_iota(jnp.int32, (n8, 8, HGRN_W), 1)
    for d in (1, 2, 4):
        w = w + jnp.where(sub8 >= d, pltpu.roll(w, d, axis=1), 0.0)
    w = w.reshape(CHUNK, HGRN_W)
    parts = []
    for j in range(N_SUB):
        lo = w[SUB * j:SUB * j + 8, :]
        hi = w[SUB * j + 8:SUB * j + 16, :] + lo[7:8, :]
        parts += [lo, hi]
    return jnp.concatenate(parts, axis=0)


def _hgrn_chunk_head(r0, hd, c2_ref, lk_ref, q_ref, v_ref, st_ref, e_tot, e_pre, decay):
    hs = _head(hd)
    rows = slice(r0, r0 + CHUNK)
    c2 = c2_ref[hd, rows, :]
    lk = lk_ref[hd, rows, :]
    q = q_ref[hd, rows, :]
    v_b = v_ref[hd, rows, :].astype(BF16)

    def sub(a, j):
        return a[SUB * j:SUB * (j + 1), :]

    tot = [c2[SUB * j + SUB - 1:SUB * j + SUB, :] for j in range(N_SUB)]
    tot_own = jnp.concatenate([jnp.broadcast_to(t, (SUB, HEAD_DIM)) for t in tot], axis=0)
    qh = q * jnp.exp2(c2)
    kh = jnp.exp2(lk + tot_own)
    et = [None] + [e_tot[j][:, hs] for j in range(1, N_SUB)]
    qg = jnp.concatenate([sub(qh, 0)] + [sub(qh, j) * e_pre[j][:, hs] for j in range(1, N_SUB)],
                         axis=0).astype(BF16)
    qh_b = qh.astype(BF16)

    zero = jnp.zeros((SUB, HEAD_DIM), F32)
    k2_0 = sub(kh, 0) * et[1]
    k3_0 = k2_0 * et[2]
    k3_1 = sub(kh, 1) * et[2]
    k1 = jnp.concatenate([sub(kh, 0), zero, zero, zero], axis=0).astype(BF16)
    k2 = jnp.concatenate([k2_0, sub(kh, 1), zero, zero], axis=0).astype(BF16)
    k3 = jnp.concatenate([k3_0, k3_1, sub(kh, 2), zero], axis=0).astype(BF16)
    kg = jnp.concatenate([k3_0 * et[3], k3_1 * et[3], sub(kh, 2) * et[3], sub(kh, 3)],
                         axis=0).astype(BF16)

    a1 = _dot_nt(qh_b[SUB:2 * SUB, :], k1)
    a2 = _dot_nt(qh_b[2 * SUB:3 * SUB, :], k2)
    a3 = _dot_nt(qh_b[3 * SUB:4 * SUB, :], k3)
    a_cross = jnp.concatenate([jnp.zeros((SUB, CHUNK), F32), a1, a2, a3], axis=0)

    lane = lax.broadcasted_iota(jnp.int32, (8, CHUNK), 1)
    row8 = lax.broadcasted_iota(jnp.int32, (8, CHUNK), 0)
    blocks = []
    for j in range(N_SUB):
        base = r0 + SUB * j
        c_lo, c_hi = c2[SUB * j:SUB * j + 8, :], c2[SUB * j + 8:SUB * j + 16, :]
        q_lo, q_hi = q[SUB * j:SUB * j + 8, :], q[SUB * j + 8:SUB * j + 16, :]
        d_lo = jnp.zeros((8, CHUNK), F32)
        d_hi = jnp.zeros((8, CHUNK), F32)
        for s in range(SUB):
            lks = jnp.broadcast_to(lk_ref[hd, base + s:base + s + 1, :], (8, HEAD_DIM))
            ex_hi = c_hi + lks if s < 8 else jnp.minimum(c_hi + lks, 0.0)
            col_hi = jnp.sum(q_hi * jnp.exp2(ex_hi), axis=-1, keepdims=True)
            d_hi = jnp.where(lane == SUB * j + s, col_hi, d_hi)
            if s < 8:
                col_lo = jnp.sum(q_lo * jnp.exp2(jnp.minimum(c_lo + lks, 0.0)), axis=-1, keepdims=True)
                d_lo = jnp.where(lane == SUB * j + s, col_lo, d_lo)
        d_lo = jnp.where(lane <= SUB * j + row8, d_lo, 0.0)
        d_hi = jnp.where(lane <= SUB * j + 8 + row8, d_hi, 0.0)
        blocks += [d_lo, d_hi]
    a = (a_cross + jnp.concatenate(blocks, axis=0)).astype(BF16)
    o_intra = _dot(a, v_b)

    st = st_ref[hd]
    o_state = _dot_nt(qg, st.astype(BF16))
    st_ref[hd] = decay[:, hs] * st + _dot_tn(v_b, kg)
    return o_state + o_intra


def _mixer_kernel(x_ref, g_ref, win_ref, poolw_ref, pscale_ref, lbl_ref, hnorm_ref, wpp_ref, whp_ref,
                  wout_ref, o_ref, h_ref, c2_ref, lk_ref, q_ref, v_ref, ext_ref, oh_ref, og_ref, ga_ref, gb_ref,
                  st_ref, *, layer):
    tb = pl.program_id(1)

    @pl.when(tb == 0)
    def _():
        st_ref[...] = jnp.zeros_like(st_ref)
        ext_ref[0:POOL_CARRY, :] = jnp.zeros((POOL_CARRY, POOL_WIDTH), F32)

    x = x_ref[...]
    h_ref[...] = _rmsnorm(x, g_ref[...]).astype(BF16)
    h = h_ref[...]

    lb = _lower_bound(lbl_ref, layer)
    log_lb = jnp.log(lb)
    log1m_lb = jnp.log1p(-lb)

    fz = _dot(h, win_ref[:, OFF_F:OFF_F + HGRN_W])
    l1p = jnp.log(1.0 + jnp.exp(-jnp.abs(fz)))
    bb = log1m_lb + (jnp.minimum(fz, 0.0) - l1p)
    lf2 = (jnp.maximum(log_lb, bb) + jnp.log(1.0 + jnp.exp(-jnp.abs(log_lb - bb)))) * LOG2E
    lk2 = (log1m_lb - jnp.maximum(fz, 0.0) - l1p) * LOG2E
    qz = _dot(h, win_ref[:, OFF_Q:OFF_Q + HGRN_W])
    q = qz * _sigmoid(qz)
    vz = _dot(h, win_ref[:, OFF_I:OFF_I + HGRN_W])
    for hd in range(HEADS):
        q_ref[hd] = q[:, _head(hd)]
        v_ref[hd] = vz[:, _head(hd)]

    def proj(col0):
        return _dot(h_ref[...], win_ref[:, col0:col0 + FF_TILE])

    def pool_job(t):
        def run():
            ext_ref[POOL_CARRY:POOL_CARRY + TM_MIX, t * FF_TILE:(t + 1) * FF_TILE] = proj(OFF_POOL + t * FF_TILE)
        return run

    def og_job(t):
        def run():
            og = proj(OFF_OG + t * FF_TILE)
            og_ref[:, t * FF_TILE:(t + 1) * FF_TILE] = og * _sigmoid(og)
        return run

    def gate_job(dst_ref, off, t):
        def run():
            dst_ref[:, t * FF_TILE:(t + 1) * FF_TILE] = _sigmoid(proj(off + t * FF_TILE))
        return run

    side_jobs = ([pool_job(t) for t in range(POOL_WIDTH // FF_TILE)]
                 + [og_job(t) for t in range(HGRN_W // FF_TILE)]
                 + [gate_job(ga_ref, OFF_GA, t) for t in range(D_MODEL // FF_TILE)]
                 + [gate_job(gb_ref, OFF_GB, t) for t in range(D_MODEL // FF_TILE)])
    n_chunks = TM_MIX // CHUNK

    for ci in range(n_chunks):
        for job in side_jobs[ci * len(side_jobs) // n_chunks:(ci + 1) * len(side_jobs) // n_chunks]:
            job()
        r0 = ci * CHUNK
        c2 = _sub_scan(lf2[r0:r0 + CHUNK, :])
        lk = lk2[r0:r0 + CHUNK, :] - c2
        for hd in range(HEADS):
            c2_ref[hd, r0:r0 + CHUNK, :] = c2[:, _head(hd)]
            lk_ref[hd, r0:r0 + CHUNK, :] = lk[:, _head(hd)]
        tot = [c2[SUB * j + SUB - 1:SUB * j + SUB, :] for j in range(N_SUB)]
        e_tot = [None] + [jnp.exp2(tot[j]) for j in range(1, N_SUB)]
        p1 = tot[0]
        p2 = p1 + tot[1]
        p3 = p2 + tot[2]
        e_pre = [None, jnp.exp2(p1), jnp.exp2(p2), jnp.exp2(p3)]
        decay = jnp.exp2(p3 + tot[3])
        outs = [_hgrn_chunk_head(r0, hd, c2_ref, lk_ref, q_ref, v_ref, st_ref, e_tot, e_pre, decay)
                for hd in range(HEADS)]
        oh_ref[r0:r0 + CHUNK, :] = jnp.concatenate(outs, axis=-1)

    o = oh_ref[...]
    normed = []
    for hd in range(HEADS):
        oh = o[:, _head(hd)]
        ms = jnp.mean(oh * oh, axis=-1, keepdims=True)
        normed.append(oh * lax.rsqrt(ms + EPS))
    on = jnp.concatenate(normed, axis=-1) * hnorm_ref[...]
    hgrn_out = (on * og_ref[...]).astype(BF16)
    pb = _dot(hgrn_out, whp_ref[...])

    pos = lax.broadcasted_iota(jnp.int32, (POOL_CARRY, POOL_GROUP_DIM), 0) + (tb * TM_MIX + 1)
    mixed = []
    for gi, w in enumerate(POOL_WINDOWS):
        cols = slice(gi * POOL_GROUP_DIM, (gi + 1) * POOL_GROUP_DIM)
        u = ext_ref[POOL_CARRY:POOL_CARRY + TM_MIX, cols]
        wsum = u
        for j in range(1, w):
            wsum = wsum + ext_ref[POOL_CARRY - j:POOL_CARRY - j + TM_MIX, cols]
        inv_head = 1.0 / jnp.minimum(pos, w).astype(F32)
        mean = jnp.concatenate([wsum[:POOL_CARRY] * inv_head, wsum[POOL_CARRY:] * (1.0 / w)], axis=0)
        mixed.append(_dot((mean - u).astype(BF16), poolw_ref[gi]))
    pool_out = (jnp.concatenate(mixed, axis=-1) * pscale_ref[...]).astype(BF16)
    ext_ref[0:POOL_CARRY, :] = ext_ref[TM_MIX:TM_MIX + POOL_CARRY, :]
    pa = _dot(pool_out, wpp_ref[...])

    merged = (ga_ref[...] * pa + gb_ref[...] * pb).astype(BF16)
    o_ref[...] = x + _dot(merged, wout_ref[...])


def _mixer(x, g, win, poolw, pscale, lbl, hnorm, wpp, whp, wout, layer, batch):
    n = x.shape[0]
    nt = n // batch // TM_MIX
    depth = lbl.shape[0]
    return pl.pallas_call(
        functools.partial(_mixer_kernel, layer=layer),
        grid=(batch, nt),
        in_specs=[
            pl.BlockSpec((TM_MIX, D_MODEL), lambda b, t: (b * nt + t, 0)),
            _const_spec((1, D_MODEL)),
            _const_spec((D_MODEL, D_IN)),
            _const_spec((len(POOL_WINDOWS), POOL_GROUP_DIM, POOL_GROUP_DIM)),
            _const_spec((1, POOL_WIDTH)),
            _const_spec((depth, HGRN_W)),
            _const_spec((1, HGRN_W)),
            _const_spec((POOL_WIDTH, D_MODEL)),
            _const_spec((HGRN_W, D_MODEL)),
            _const_spec((D_MODEL, D_MODEL)),
        ],
        out_specs=pl.BlockSpec((TM_MIX, D_MODEL), lambda b, t: (b * nt + t, 0)),
        out_shape=jax.ShapeDtypeStruct((n, D_MODEL), F32),
        scratch_shapes=[
            pltpu.VMEM((TM_MIX, D_MODEL), BF16),
            pltpu.VMEM((HEADS, TM_MIX, HEAD_DIM), F32),
            pltpu.VMEM((HEADS, TM_MIX, HEAD_DIM), F32),
            pltpu.VMEM((HEADS, TM_MIX, HEAD_DIM), F32),
            pltpu.VMEM((HEADS, TM_MIX, HEAD_DIM), F32),
            pltpu.VMEM((POOL_CARRY + TM_MIX, POOL_WIDTH), F32),
            pltpu.VMEM((TM_MIX, HGRN_W), F32),
            pltpu.VMEM((TM_MIX, HGRN_W), F32),
            pltpu.VMEM((TM_MIX, D_MODEL), F32),
            pltpu.VMEM((TM_MIX, D_MODEL), F32),
            pltpu.VMEM((HEADS, HEAD_DIM, HEAD_DIM), F32),
        ],
        compiler_params=pltpu.CompilerParams(
            dimension_semantics=("arbitrary", "arbitrary"), vmem_limit_bytes=VMEM_LIMIT_BYTES),
        name="hybrid_mixer",
    )(x, g, win, poolw, pscale, lbl, hnorm, wpp, whp, wout)


def kernel(x, ffn1_norm, ffn1_w_gate, ffn1_w_up, ffn1_w_down, mix_norm, w_in, pool_w, pool_scale,
           lb_logits, hgrn_norm, w_pool_proj, w_hgrn_proj, w_out, ffn2_norm, ffn2_w_gate, ffn2_w_up,
           ffn2_w_down, final_norm):
    batch, seq, _ = x.shape
    depth = ffn1_norm.shape[0]
    xf = x.reshape(batch * seq, D_MODEL)
    fg = final_norm.reshape(1, D_MODEL)
    for l in range(depth):
        xf = _ffn(xf, ffn1_norm[l].reshape(1, D_MODEL), ffn1_w_gate[l].astype(BF16),
                  ffn1_w_up[l].astype(BF16), ffn1_w_down[l].astype(BF16), fg, False)
        xf = _mixer(xf, mix_norm[l].reshape(1, D_MODEL), w_in[l].astype(BF16), pool_w[l].astype(BF16),
                    pool_scale[l].reshape(1, POOL_WIDTH), lb_logits, hgrn_norm[l].reshape(1, HGRN_W),
                    w_pool_proj[l].astype(BF16), w_hgrn_proj[l].astype(BF16), w_out[l].astype(BF16),
                    l, batch)
        xf = _ffn(xf, ffn2_norm[l].reshape(1, D_MODEL), ffn2_w_gate[l].astype(BF16),
                  ffn2_w_up[l].astype(BF16), ffn2_w_down[l].astype(BF16), fg, l == depth - 1)
    return xf.reshape(batch, seq, D_MODEL)
```

```python
import functools

import jax
import jax.numpy as jnp
from jax import lax
from jax.experimental import pallas as pl
from jax.experimental.pallas import tpu as pltpu

D_MODEL = 1024
D_FF = 2816
CHUNK = 64
SUB = 16
N_SUB = CHUNK // SUB
POOL_WIDTH = 512
POOL_WINDOWS = (2, 4, 8, 16)
POOL_GROUP_DIM = 128
POOL_CARRY = 16
HEADS = 4
HEAD_DIM = 128
HGRN_W = HEADS * HEAD_DIM
EPS = 1e-6
LOG2E = 1.4426950408889634

FF_TILE = 256
N_FF_TILES = D_FF // FF_TILE
TM_FFN = 512
TM_MIX = 512
CAST_ROWS_FF_IN = 128
CAST_ROWS_FF_OUT = 352
CAST_ROWS_IN_PROJ = 64
CAST_ROWS_SQUARE = 256
VMEM_LIMIT_BYTES = 56 * 1024 * 1024

OFF_POOL = 0
OFF_Q = POOL_WIDTH
OFF_F = OFF_Q + HGRN_W
OFF_I = OFF_F + HGRN_W
OFF_OG = OFF_I + HGRN_W
OFF_GA = OFF_OG + HGRN_W
OFF_GB = OFF_GA + D_MODEL
D_IN = OFF_GB + D_MODEL

F32 = jnp.float32
BF16 = jnp.bfloat16


def _dot(a, b):
    return jnp.dot(a, b, preferred_element_type=F32)


def _dot_nt(a, b):
    return lax.dot_general(a, b, (((1,), (1,)), ((), ())), preferred_element_type=F32)


def _dot_tn(a, b):
    return lax.dot_general(a, b, (((0,), (0,)), ((), ())), preferred_element_type=F32)


def _rmsnorm(x, g):
    ms = jnp.mean(x * x, axis=-1, keepdims=True)
    return x * lax.rsqrt(ms + EPS) * g


def _sigmoid(x):
    return 1.0 / (1.0 + jnp.exp(-x))


def _weight_chunk_copy(src_hbm, layer, stage_ref, sem_ref, c):
    rows = stage_ref.shape[1]
    return pltpu.make_async_copy(src_hbm.at[layer, pl.ds(c * rows, rows), :], stage_ref.at[c % 2],
                                 sem_ref.at[c % 2])


def _cast_weight(src_hbm, layer, dst_ref, stage_ref, sem_ref):
    rows = stage_ref.shape[1]
    n_chunks, rem = divmod(dst_ref.shape[0], rows)
    assert rem == 0
    _weight_chunk_copy(src_hbm, layer, stage_ref, sem_ref, 0).start()
    for c in range(n_chunks):
        if c + 1 < n_chunks:
            _weight_chunk_copy(src_hbm, layer, stage_ref, sem_ref, c + 1).start()
        _weight_chunk_copy(src_hbm, layer, stage_ref, sem_ref, c).wait()
        dst_ref[c * rows:(c + 1) * rows, :] = stage_ref[c % 2].astype(BF16)


def _ffn_kernel(x_ref, g_ref, wg_hbm, wu_hbm, wd_hbm, fg_ref, o_ref, h_ref, a_ref, wg_ref, wu_ref, wd_ref,
                stage_in_ref, stage_out_ref, sem_ref, *, layer, final_norm):
    @pl.when(pl.program_id(0) == 0)
    def _():
        _cast_weight(wg_hbm, layer, wg_ref, stage_in_ref, sem_ref)
        _cast_weight(wu_hbm, layer, wu_ref, stage_in_ref, sem_ref)
        _cast_weight(wd_hbm, layer, wd_ref, stage_out_ref, sem_ref)

    x = x_ref[...]
    h_ref[...] = _rmsnorm(x, g_ref[...]).astype(BF16)
    for j in range(N_FF_TILES):
        cols = slice(j * FF_TILE, (j + 1) * FF_TILE)
        h = h_ref[...]
        g = _dot(h, wg_ref[:, cols])
        u = _dot(h, wu_ref[:, cols])
        a_ref[:, cols] = (g * _sigmoid(g) * u).astype(BF16)
    y = x + 0.5 * _dot(a_ref[...], wd_ref[...])
    if final_norm:
        y = _rmsnorm(y, fg_ref[...])
    o_ref[...] = y


def _const_spec(shape):
    nd = len(shape)
    return pl.BlockSpec(shape, lambda *_: (0,) * nd, pipeline_mode=pl.Buffered(1))


def _ffn(x, g, wg, wu, wd, fg, layer, final_norm):
    n = x.shape[0]
    hbm = pl.BlockSpec(memory_space=pl.ANY)
    return pl.pallas_call(
        functools.partial(_ffn_kernel, layer=layer, final_norm=final_norm),
        grid=(n // TM_FFN,),
        in_specs=[
            pl.BlockSpec((TM_FFN, D_MODEL), lambda i: (i, 0)),
            _const_spec((1, D_MODEL)),
            hbm, hbm, hbm,
            _const_spec((1, D_MODEL)),
        ],
        out_specs=pl.BlockSpec((TM_FFN, D_MODEL), lambda i: (i, 0)),
        out_shape=jax.ShapeDtypeStruct((n, D_MODEL), F32),
        scratch_shapes=[
            pltpu.VMEM((TM_FFN, D_MODEL), BF16),
            pltpu.VMEM((TM_FFN, D_FF), BF16),
            pltpu.VMEM((D_MODEL, D_FF), BF16),
            pltpu.VMEM((D_MODEL, D_FF), BF16),
            pltpu.VMEM((D_FF, D_MODEL), BF16),
            pltpu.VMEM((2, CAST_ROWS_FF_IN, D_FF), F32),
            pltpu.VMEM((2, CAST_ROWS_FF_OUT, D_MODEL), F32),
            pltpu.SemaphoreType.DMA((2,)),
        ],
        compiler_params=pltpu.CompilerParams(
            dimension_semantics=("arbitrary",), vmem_limit_bytes=VMEM_LIMIT_BYTES),
        name="swiglu_half_step",
    )(x, g, wg, wu, wd, fg)


def _lower_bound(lbl_ref, layer):
    logits = lbl_ref[...]
    m = jnp.max(logits, axis=0, keepdims=True)
    e = jnp.exp(logits - m)
    p = e / jnp.sum(e, axis=0, keepdims=True)
    lb = jnp.zeros((1, HGRN_W), F32)
    for j in range(1, layer + 1):
        lb = lb + p[j:j + 1, :]
    return lb


def _head(hd):
    return slice(hd * HEAD_DIM, (hd + 1) * HEAD_DIM)


def _sub_scan(lf2):
    n8 = CHUNK // 8
    w = lf2.reshape(n8, 8, HGRN_W)
    sub8 = lax.broadcasted_iota(jnp.int32, (n8, 8, HGRN_W), 1)
    for d in (1, 2, 4):
        w = w + jnp.where(sub8 >= d, pltpu.roll(w, d, axis=1), 0.0)
    w = w.reshape(CHUNK, HGRN_W)
    parts = []
    for j in range(N_SUB):
        lo = w[SUB * j:SUB * j + 8, :]
        hi = w[SUB * j + 8:SUB * j + 16, :] + lo[7:8, :]
        parts += [lo, hi]
    return jnp.concatenate(parts, axis=0)


def _hgrn_chunk_head(r0, hd, c2_ref, lk_ref, q_ref, v_ref, st_ref, e_tot, e_pre, decay):
    hs = _head(hd)
    rows = slice(r0, r0 + CHUNK)
    c2 = c2_ref[hd, rows, :]
    lk = lk_ref[hd, rows, :]
    q = q_ref[hd, rows, :]
    v_b = v_ref[hd, rows, :].astype(BF16)

    def sub(a, j):
        return a[SUB * j:SUB * (j + 1), :]

    tot = [c2[SUB * j + SUB - 1:SUB * j + SUB, :] for j in range(N_SUB)]
    tot_own = jnp.concatenate([jnp.broadcast_to(t, (SUB, HEAD_DIM)) for t in tot], axis=0)
    qh = q * jnp.exp2(c2)
    kh = jnp.exp2(lk + tot_own)
    et = [None] + [e_tot[j][:, hs] for j in range(1, N_SUB)]
    qg = jnp.concatenate([sub(qh, 0)] + [sub(qh, j) * e_pre[j][:, hs] for j in range(1, N_SUB)],
                         axis=0).astype(BF16)
    qh_b = qh.astype(BF16)

    zero = jnp.zeros((SUB, HEAD_DIM), F32)
    k2_0 = sub(kh, 0) * et[1]
    k3_0 = k2_0 * et[2]
    k3_1 = sub(kh, 1) * et[2]
    k1 = jnp.concatenate([sub(kh, 0), zero, zero, zero], axis=0).astype(BF16)
    k2 = jnp.concatenate([k2_0, sub(kh, 1), zero, zero], axis=0).astype(BF16)
    k3 = jnp.concatenate([k3_0, k3_1, sub(kh, 2), zero], axis=0).astype(BF16)
    kg = jnp.concatenate([k3_0 * et[3], k3_1 * et[3], sub(kh, 2) * et[3], sub(kh, 3)],
                         axis=0).astype(BF16)

    a1 = _dot_nt(qh_b[SUB:2 * SUB, :], k1)
    a2 = _dot_nt(qh_b[2 * SUB:3 * SUB, :], k2)
    a3 = _dot_nt(qh_b[3 * SUB:4 * SUB, :], k3)
    a_cross = jnp.concatenate([jnp.zeros((SUB, CHUNK), F32), a1, a2, a3], axis=0)

    lane = lax.broadcasted_iota(jnp.int32, (8, CHUNK), 1)
    row8 = lax.broadcasted_iota(jnp.int32, (8, CHUNK), 0)
    blocks = []
    for j in range(N_SUB):
        base = r0 + SUB * j
        c_lo, c_hi = c2[SUB * j:SUB * j + 8, :], c2[SUB * j + 8:SUB * j + 16, :]
        q_lo, q_hi = q[SUB * j:SUB * j + 8, :], q[SUB * j + 8:SUB * j + 16, :]
        d_lo = jnp.zeros((8, CHUNK), F32)
        d_hi = jnp.zeros((8, CHUNK), F32)
        for s in range(SUB):
            lks = jnp.broadcast_to(lk_ref[hd, base + s:base + s + 1, :], (8, HEAD_DIM))
            ex_hi = c_hi + lks if s < 8 else jnp.minimum(c_hi + lks, 0.0)
            col_hi = jnp.sum(q_hi * jnp.exp2(ex_hi), axis=-1, keepdims=True)
            d_hi = jnp.where(lane == SUB * j + s, col_hi, d_hi)
            if s < 8:
                col_lo = jnp.sum(q_lo * jnp.exp2(jnp.minimum(c_lo + lks, 0.0)), axis=-1, keepdims=True)
                d_lo = jnp.where(lane == SUB * j + s, col_lo, d_lo)
        d_lo = jnp.where(lane <= SUB * j + row8, d_lo, 0.0)
        d_hi = jnp.where(lane <= SUB * j + 8 + row8, d_hi, 0.0)
        blocks += [d_lo, d_hi]
    a = (a_cross + jnp.concatenate(blocks, axis=0)).astype(BF16)
    o_intra = _dot(a, v_b)

    st = st_ref[hd]
    o_state = _dot_nt(qg, st.astype(BF16))
    st_ref[hd] = decay[:, hs] * st + _dot_tn(v_b, kg)
    return o_state + o_intra


def _mixer_kernel(x_ref, g_ref, win_hbm, poolw_ref, pscale_ref, lbl_ref, hnorm_ref, wpp_hbm, whp_hbm,
                  wout_hbm, o_ref, h_ref, c2_ref, lk_ref, q_ref, v_ref, ext_ref, oh_ref, og_ref, ga_ref, gb_ref,
                  st_ref, win_ref, wpp_ref, whp_ref, wout_ref, stage_in_ref, stage_sq_ref, sem_ref, *, layer):
    tb = pl.program_id(1)

    @pl.when((pl.program_id(0) == 0) & (tb == 0))
    def _():
        _cast_weight(win_hbm, layer, win_ref, stage_in_ref, sem_ref)
        _cast_weight(wpp_hbm, layer, wpp_ref, stage_sq_ref, sem_ref)
        _cast_weight(whp_hbm, layer, whp_ref, stage_sq_ref, sem_ref)
        _cast_weight(wout_hbm, layer, wout_ref, stage_sq_ref, sem_ref)

    @pl.when(tb == 0)
    def _():
        st_ref[...] = jnp.zeros_like(st_ref)
        ext_ref[0:POOL_CARRY, :] = jnp.zeros((POOL_CARRY, POOL_WIDTH), F32)

    x = x_ref[...]
    h_ref[...] = _rmsnorm(x, g_ref[...]).astype(BF16)
    h = h_ref[...]

    lb = _lower_bound(lbl_ref, layer)
    log_lb = jnp.log(lb)
    log1m_lb = jnp.log1p(-lb)

    fz = _dot(h, win_ref[:, OFF_F:OFF_F + HGRN_W])
    l1p = jnp.log(1.0 + jnp.exp(-jnp.abs(fz)))
    bb = log1m_lb + (jnp.minimum(fz, 0.0) - l1p)
    lf2 = (jnp.maximum(log_lb, bb) + jnp.log(1.0 + jnp.exp(-jnp.abs(log_lb - bb)))) * LOG2E
    lk2 = (log1m_lb - jnp.maximum(fz, 0.0) - l1p) * LOG2E
    qz = _dot(h, win_ref[:, OFF_Q:OFF_Q + HGRN_W])
    q = qz * _sigmoid(qz)
    vz = _dot(h, win_ref[:, OFF_I:OFF_I + HGRN_W])
    for hd in range(HEADS):
        q_ref[hd] = q[:, _head(hd)]
        v_ref[hd] = vz[:, _head(hd)]

    def proj(col0):
        return _dot(h_ref[...], win_ref[:, col0:col0 + FF_TILE])

    def pool_job(t):
        def run():
            ext_ref[POOL_CARRY:POOL_CARRY + TM_MIX, t * FF_TILE:(t + 1) * FF_TILE] = proj(OFF_POOL + t * FF_TILE)
        return run

    def og_job(t):
        def run():
            og = proj(OFF_OG + t * FF_TILE)
            og_ref[:, t * FF_TILE:(t + 1) * FF_TILE] = og * _sigmoid(og)
        return run

    def gate_job(dst_ref, off, t):
        def run():
            dst_ref[:, t * FF_TILE:(t + 1) * FF_TILE] = _sigmoid(proj(off + t * FF_TILE))
        return run

    side_jobs = ([pool_job(t) for t in range(POOL_WIDTH // FF_TILE)]
                 + [og_job(t) for t in range(HGRN_W // FF_TILE)]
                 + [gate_job(ga_ref, OFF_GA, t) for t in range(D_MODEL // FF_TILE)]
                 + [gate_job(gb_ref, OFF_GB, t) for t in range(D_MODEL // FF_TILE)])
    n_chunks = TM_MIX // CHUNK

    for ci in range(n_chunks):
        for job in side_jobs[ci * len(side_jobs) // n_chunks:(ci + 1) * len(side_jobs) // n_chunks]:
            job()
        r0 = ci * CHUNK
        c2 = _sub_scan(lf2[r0:r0 + CHUNK, :])
        lk = lk2[r0:r0 + CHUNK, :] - c2
        for hd in range(HEADS):
            c2_ref[hd, r0:r0 + CHUNK, :] = c2[:, _head(hd)]
            lk_ref[hd, r0:r0 + CHUNK, :] = lk[:, _head(hd)]
        tot = [c2[SUB * j + SUB - 1:SUB * j + SUB, :] for j in range(N_SUB)]
        e_tot = [None] + [jnp.exp2(tot[j]) for j in range(1, N_SUB)]
        p1 = tot[0]
        p2 = p1 + tot[1]
        p3 = p2 + tot[2]
        e_pre = [None, jnp.exp2(p1), jnp.exp2(p2), jnp.exp2(p3)]
        decay = jnp.exp2(p3 + tot[3])
        outs = [_hgrn_chunk_head(r0, hd, c2_ref, lk_ref, q_ref, v_ref, st_ref, e_tot, e_pre, decay)
                for hd in range(HEADS)]
        oh_ref[r0:r0 + CHUNK, :] = jnp.concatenate(outs, axis=-1)

    o = oh_ref[...]
    normed = []
    for hd in range(HEADS):
        oh = o[:, _head(hd)]
        ms = jnp.mean(oh * oh, axis=-1, keepdims=True)
        normed.append(oh * lax.rsqrt(ms + EPS))
    on = jnp.concatenate(normed, axis=-1) * hnorm_ref[...]
    hgrn_out = (on * og_ref[...]).astype(BF16)
    pb = _dot(hgrn_out, whp_ref[...])

    pos = lax.broadcasted_iota(jnp.int32, (POOL_CARRY, POOL_GROUP_DIM), 0) + (tb * TM_MIX + 1)
    mixed = []
    for gi, w in enumerate(POOL_WINDOWS):
        cols = slice(gi * POOL_GROUP_DIM, (gi + 1) * POOL_GROUP_DIM)
        u = ext_ref[POOL_CARRY:POOL_CARRY + TM_MIX, cols]
        wsum = u
        for j in range(1, w):
            wsum = wsum + ext_ref[POOL_CARRY - j:POOL_CARRY - j + TM_MIX, cols]
        inv_head = 1.0 / jnp.minimum(pos, w).astype(F32)
        mean = jnp.concatenate([wsum[:POOL_CARRY] * inv_head, wsum[POOL_CARRY:] * (1.0 / w)], axis=0)
        mixed.append(_dot((mean - u).astype(BF16), poolw_ref[gi].astype(BF16)))
    pool_out = (jnp.concatenate(mixed, axis=-1) * pscale_ref[...]).astype(BF16)
    ext_ref[0:POOL_CARRY, :] = ext_ref[TM_MIX:TM_MIX + POOL_CARRY, :]
    pa = _dot(pool_out, wpp_ref[...])

    merged = (ga_ref[...] * pa + gb_ref[...] * pb).astype(BF16)
    o_ref[...] = x + _dot(merged, wout_ref[...])


def _mixer(x, g, win, poolw, pscale, lbl, hnorm, wpp, whp, wout, layer, batch):
    n = x.shape[0]
    nt = n // batch // TM_MIX
    depth = lbl.shape[0]
    hbm = pl.BlockSpec(memory_space=pl.ANY)
    return pl.pallas_call(
        functools.partial(_mixer_kernel, layer=layer),
        grid=(batch, nt),
        in_specs=[
            pl.BlockSpec((TM_MIX, D_MODEL), lambda b, t: (b * nt + t, 0)),
            _const_spec((1, D_MODEL)),
            hbm,
            _const_spec((len(POOL_WINDOWS), POOL_GROUP_DIM, POOL_GROUP_DIM)),
            _const_spec((1, POOL_WIDTH)),
            _const_spec((depth, HGRN_W)),
            _const_spec((1, HGRN_W)),
            hbm, hbm, hbm,
        ],
        out_specs=pl.BlockSpec((TM_MIX, D_MODEL), lambda b, t: (b * nt + t, 0)),
        out_shape=jax.ShapeDtypeStruct((n, D_MODEL), F32),
        scratch_shapes=[
            pltpu.VMEM((TM_MIX, D_MODEL), BF16),
            pltpu.VMEM((HEADS, TM_MIX, HEAD_DIM), F32),
            pltpu.VMEM((HEADS, TM_MIX, HEAD_DIM), F32),
            pltpu.VMEM((HEADS, TM_MIX, HEAD_DIM), F32),
            pltpu.VMEM((HEADS, TM_MIX, HEAD_DIM), F32),
            pltpu.VMEM((POOL_CARRY + TM_MIX, POOL_WIDTH), F32),
            pltpu.VMEM((TM_MIX, HGRN_W), F32),
            pltpu.VMEM((TM_MIX, HGRN_W), F32),
            pltpu.VMEM((TM_MIX, D_MODEL), F32),
            pltpu.VMEM((TM_MIX, D_MODEL), F32),
            pltpu.VMEM((HEADS, HEAD_DIM, HEAD_DIM), F32),
            pltpu.VMEM((D_MODEL, D_IN), BF16),
            pltpu.VMEM((POOL_WIDTH, D_MODEL), BF16),
            pltpu.VMEM((HGRN_W, D_MODEL), BF16),
            pltpu.VMEM((D_MODEL, D_MODEL), BF16),
            pltpu.VMEM((2, CAST_ROWS_IN_PROJ, D_IN), F32),
            pltpu.VMEM((2, CAST_ROWS_SQUARE, D_MODEL), F32),
            pltpu.SemaphoreType.DMA((2,)),
        ],
        compiler_params=pltpu.CompilerParams(
            dimension_semantics=("arbitrary", "arbitrary"), vmem_limit_bytes=VMEM_LIMIT_BYTES),
        name="hybrid_mixer",
    )(x, g, win, poolw, pscale, lbl, hnorm, wpp, whp, wout)


def kernel(x, ffn1_norm, ffn1_w_gate, ffn1_w_up, ffn1_w_down, mix_norm, w_in, pool_w, pool_scale,
           lb_logits, hgrn_norm, w_pool_proj, w_hgrn_proj, w_out, ffn2_norm, ffn2_w_gate, ffn2_w_up,
           ffn2_w_down, final_norm):
    batch, seq, _ = x.shape
    depth = ffn1_norm.shape[0]
    xf = x.reshape(batch * seq, D_MODEL)
    fg = final_norm.reshape(1, D_MODEL)
    for l in range(depth):
        xf = _ffn(xf, ffn1_norm[l].reshape(1, D_MODEL), ffn1_w_gate, ffn1_w_up, ffn1_w_down, fg, l, False)
        xf = _mixer(xf, mix_norm[l].reshape(1, D_MODEL), w_in, pool_w[l],
                    pool_scale[l].reshape(1, POOL_WIDTH), lb_logits, hgrn_norm[l].reshape(1, HGRN_W),
                    w_pool_proj, w_hgrn_proj, w_out, l, batch)
        xf = _ffn(xf, ffn2_norm[l].reshape(1, D_MODEL), ffn2_w_gate, ffn2_w_up, ffn2_w_down, fg, l,
                  l == depth - 1)
    return xf.reshape(batch, seq, D_MODEL)
```

```python
import functools

import jax
import jax.numpy as jnp
from jax import lax
from jax.experimental import pallas as pl
from jax.experimental.pallas import tpu as pltpu

D_MODEL = 1024
D_FF = 2816
CHUNK = 64
SUB = 16
N_SUB = CHUNK // SUB
POOL_WIDTH = 512
POOL_WINDOWS = (2, 4, 8, 16)
POOL_GROUP_DIM = 128
POOL_CARRY = 16
HEADS = 4
HEAD_DIM = 128
HGRN_W = HEADS * HEAD_DIM
EPS = 1e-6
LOG2E = 1.4426950408889634

FF_TILE = 256
N_FF_TILES = D_FF // FF_TILE
TM_FFN = 512
TM_MIX = 512
CAST_ROWS_FF_IN = 128
CAST_ROWS_FF_OUT = 352
CAST_ROWS_IN_PROJ = 64
CAST_ROWS_SQUARE = 256
VMEM_LIMIT_BYTES = 56 * 1024 * 1024

OFF_POOL = 0
OFF_Q = POOL_WIDTH
OFF_F = OFF_Q + HGRN_W
OFF_I = OFF_F + HGRN_W
OFF_OG = OFF_I + HGRN_W
OFF_GA = OFF_OG + HGRN_W
OFF_GB = OFF_GA + D_MODEL
D_IN = OFF_GB + D_MODEL

F32 = jnp.float32
BF16 = jnp.bfloat16


def _dot(a, b):
    return jnp.dot(a, b, preferred_element_type=F32)


def _dot_nt(a, b):
    return lax.dot_general(a, b, (((1,), (1,)), ((), ())), preferred_element_type=F32)


def _dot_tn(a, b):
    return lax.dot_general(a, b, (((0,), (0,)), ((), ())), preferred_element_type=F32)


def _rmsnorm(x, g):
    ms = jnp.mean(x * x, axis=-1, keepdims=True)
    return x * lax.rsqrt(ms + EPS) * g


def _sigmoid(x):
    return 1.0 / (1.0 + jnp.exp(-x))


def _weight_chunk_copy(src_hbm, layer, stage_ref, sem_ref, c):
    rows = stage_ref.shape[1]
    return pltpu.make_async_copy(src_hbm.at[layer, pl.ds(c * rows, rows), :], stage_ref.at[c % 2],
                                 sem_ref.at[c % 2])


def _cast_weight(src_hbm, layer, dst_ref, stage_ref, sem_ref):
    rows = stage_ref.shape[1]
    n_chunks, rem = divmod(dst_ref.shape[0], rows)
    assert rem == 0
    _weight_chunk_copy(src_hbm, layer, stage_ref, sem_ref, 0).start()
    for c in range(n_chunks):
        if c + 1 < n_chunks:
            _weight_chunk_copy(src_hbm, layer, stage_ref, sem_ref, c + 1).start()
        _weight_chunk_copy(src_hbm, layer, stage_ref, sem_ref, c).wait()
        dst_ref[c * rows:(c + 1) * rows, :] = stage_ref[c % 2].astype(BF16)


def _ffn_kernel(x_ref, g_ref, wg_hbm, wu_hbm, wd_hbm, fg_ref, o_ref, h_ref, a_ref, wg_ref, wu_ref, wd_ref,
                stage_in_ref, stage_out_ref, sem_ref, *, layer, final_norm):
    @pl.when(pl.program_id(0) == 0)
    def _():
        _cast_weight(wg_hbm, layer, wg_ref, stage_in_ref, sem_ref)
        _cast_weight(wu_hbm, layer, wu_ref, stage_in_ref, sem_ref)
        _cast_weight(wd_hbm, layer, wd_ref, stage_out_ref, sem_ref)

    x = x_ref[...]
    h_ref[...] = _rmsnorm(x, g_ref[...]).astype(BF16)
    for j in range(N_FF_TILES):
        cols = slice(j * FF_TILE, (j + 1) * FF_TILE)
        h = h_ref[...]
        g = _dot(h, wg_ref[:, cols])
        u = _dot(h, wu_ref[:, cols])
        a_ref[:, cols] = (g * _sigmoid(g) * u).astype(BF16)
    y = x + 0.5 * _dot(a_ref[...], wd_ref[...])
    if final_norm:
        y = _rmsnorm(y, fg_ref[...])
    o_ref[...] = y


def _const_spec(shape):
    nd = len(shape)
    return pl.BlockSpec(shape, lambda *_: (0,) * nd, pipeline_mode=pl.Buffered(1))


def _ffn(x, g, wg, wu, wd, fg, layer, final_norm):
    n = x.shape[0]
    hbm = pl.BlockSpec(memory_space=pl.ANY)
    return pl.pallas_call(
        functools.partial(_ffn_kernel, layer=layer, final_norm=final_norm),
        grid=(n // TM_FFN,),
        in_specs=[
            pl.BlockSpec((TM_FFN, D_MODEL), lambda i: (i, 0)),
            _const_spec((1, D_MODEL)),
            hbm, hbm, hbm,
            _const_spec((1, D_MODEL)),
        ],
        out_specs=pl.BlockSpec((TM_FFN, D_MODEL), lambda i: (i, 0)),
        out_shape=jax.ShapeDtypeStruct((n, D_MODEL), F32),
        scratch_shapes=[
            pltpu.VMEM((TM_FFN, D_MODEL), BF16),
            pltpu.VMEM((TM_FFN, D_FF), BF16),
            pltpu.VMEM((D_MODEL, D_FF), BF16),
            pltpu.VMEM((D_MODEL, D_FF), BF16),
            pltpu.VMEM((D_FF, D_MODEL), BF16),
            pltpu.VMEM((2, CAST_ROWS_FF_IN, D_FF), F32),
            pltpu.VMEM((2, CAST_ROWS_FF_OUT, D_MODEL), F32),
            pltpu.SemaphoreType.DMA((2,)),
        ],
        compiler_params=pltpu.CompilerParams(
            dimension_semantics=("arbitrary",), vmem_limit_bytes=VMEM_LIMIT_BYTES),
        name="swiglu_half_step",
    )(x, g, wg, wu, wd, fg)


def _lower_bound(lbl_ref, layer):
    logits = lbl_ref[...]
    m = jnp.max(logits, axis=0, keepdims=True)
    e = jnp.exp(logits - m)
    p = e / jnp.sum(e, axis=0, keepdims=True)
    lb = jnp.zeros((1, HGRN_W), F32)
    for j in range(1, layer + 1):
        lb = lb + p[j:j + 1, :]
    return lb


def _head(hd):
    return slice(hd * HEAD_DIM, (hd + 1) * HEAD_DIM)


def _sub_scan(lf2):
    n8 = CHUNK // 8
    w = lf2.reshape(n8, 8, HGRN_W)
    sub8 = lax.broadcasted_iota(jnp.int32, (n8, 8, HGRN_W), 1)
    for d in (1, 2, 4):
        w = w + jnp.where(sub8 >= d, pltpu.roll(w, d, axis=1), 0.0)
    w = w.reshape(CHUNK, HGRN_W)
    parts = []
    for j in range(N_SUB):
        lo = w[SUB * j:SUB * j + 8, :]
        hi = w[SUB * j + 8:SUB * j + 16, :] + lo[7:8, :]
        parts += [lo, hi]
    return jnp.concatenate(parts, axis=0)


def _hgrn_chunk_head(r0, hd, c2_ref, lk_ref, q_ref, v_ref, st_ref, e_tot, e_pre, decay):
    hs = _head(hd)
    rows = slice(r0, r0 + CHUNK)
    c2 = c2_ref[hd, rows, :]
    lk = lk_ref[hd, rows, :]
    q = q_ref[hd, rows, :]
    v_b = v_ref[hd, rows, :].astype(BF16)

    def sub(a, j):
        return a[SUB * j:SUB * (j + 1), :]

    tot = [c2[SUB * j + SUB - 1:SUB * j + SUB, :] for j in range(N_SUB)]
    tot_own = jnp.concatenate([jnp.broadcast_to(t, (SUB, HEAD_DIM)) for t in tot], axis=0)
    qh = q * jnp.exp2(c2)
    kh = jnp.exp2(lk + tot_own)
    et = [None] + [e_tot[j][:, hs] for j in range(1, N_SUB)]
    qg = jnp.concatenate([sub(qh, 0)] + [sub(qh, j) * e_pre[j][:, hs] for j in range(1, N_SUB)],
                         axis=0).astype(BF16)
    qh_b = qh.astype(BF16)

    zero = jnp.zeros((SUB, HEAD_DIM), F32)
    k2_0 = sub(kh, 0) * et[1]
    k3_0 = k2_0 * et[2]
    k3_1 = sub(kh, 1) * et[2]
    k1 = jnp.concatenate([sub(kh, 0), zero, zero, zero], axis=0).astype(BF16)
    k2 = jnp.concatenate([k2_0, sub(kh, 1), zero, zero], axis=0).astype(BF16)
    k3 = jnp.concatenate([k3_0, k3_1, sub(kh, 2), zero], axis=0).astype(BF16)
    kg = jnp.concatenate([k3_0 * et[3], k3_1 * et[3], sub(kh, 2) * et[3], sub(kh, 3)],
                         axis=0).astype(BF16)

    a1 = _dot_nt(qh_b[SUB:2 * SUB, :], k1)
    a2 = _dot_nt(qh_b[2 * SUB:3 * SUB, :], k2)
    a3 = _dot_nt(qh_b[3 * SUB:4 * SUB, :], k3)
    a_cross = jnp.concatenate([jnp.zeros((SUB, CHUNK), F32), a1, a2, a3], axis=0)

    lane = lax.broadcasted_iota(jnp.int32, (8, CHUNK), 1)
    row8 = lax.broadcasted_iota(jnp.int32, (8, CHUNK), 0)
    blocks = []
    for j in range(N_SUB):
        base = r0 + SUB * j
        c_lo, c_hi = c2[SUB * j:SUB * j + 8, :], c2[SUB * j + 8:SUB * j + 16, :]
        q_lo, q_hi = q[SUB * j:SUB * j + 8, :], q[SUB * j + 8:SUB * j + 16, :]
        d_lo = jnp.zeros((8, CHUNK), F32)
        d_hi = jnp.zeros((8, CHUNK), F32)
        for s in range(SUB):
            lks = jnp.broadcast_to(lk_ref[hd, base + s:base + s + 1, :], (8, HEAD_DIM))
            here = lane == SUB * j + s
            col_hi = jnp.sum(q_hi * jnp.exp2(c_hi + lks), axis=-1, keepdims=True)
            d_hi = jnp.where(here, col_hi, d_hi)
            if s < 8:
                col_lo = jnp.sum(q_lo * jnp.exp2(c_lo + lks), axis=-1, keepdims=True)
                d_lo = jnp.where(here, col_lo, d_lo)
        d_lo = jnp.where(lane <= SUB * j + row8, d_lo, 0.0)
        d_hi = jnp.where(lane <= SUB * j + 8 + row8, d_hi, 0.0)
        blocks += [d_lo, d_hi]
    a = (a_cross + jnp.concatenate(blocks, axis=0)).astype(BF16)
    o_intra = _dot(a, v_b)

    st = st_ref[hd]
    o_state = _dot_nt(qg, st.astype(BF16))
    st_ref[hd] = decay[:, hs] * st + _dot_tn(v_b, kg)
    return o_state + o_intra


def _mixer_kernel(x_ref, xn_ref, g_ref, win_hbm, poolw_ref, pscale_ref, lbl_ref, hnorm_ref, wpp_hbm, whp_hbm,
                  wout_hbm, o_ref, h_ref, fz_ref, qz_ref, vz_ref, c2_ref, lk_ref, q_ref, v_ref, dec_ref, ext_ref,
                  hg_ref, pm_ref, og_ref, ga_ref, gb_ref, st_ref, win_ref, wpp_ref, whp_ref, wout_ref,
                  stage_in_ref, stage_sq_ref, sem_ref, *, layer):
    tb = pl.program_id(1)
    n_chunks = TM_MIX // CHUNK

    @pl.when((pl.program_id(0) == 0) & (tb == 0))
    def _():
        _cast_weight(win_hbm, layer, win_ref, stage_in_ref, sem_ref)
        _cast_weight(wpp_hbm, layer, wpp_ref, stage_sq_ref, sem_ref)
        _cast_weight(whp_hbm, layer, whp_ref, stage_sq_ref, sem_ref)
        _cast_weight(wout_hbm, layer, wout_ref, stage_sq_ref, sem_ref)

    lb = _lower_bound(lbl_ref, layer)
    log_lb = jnp.log(lb)
    log1m_lb = jnp.log1p(-lb)

    def tile(t):
        return slice(t * FF_TILE, (t + 1) * FF_TILE)

    def norm_job(src_ref, slot):
        def run():
            h_ref[slot] = _rmsnorm(src_ref[...], g_ref[...]).astype(BF16)
        return run

    def stage_job(dst_ref, off, t, slot):
        def run():
            dst_ref[:, tile(t)] = _dot(h_ref[slot], win_ref[:, off + t * FF_TILE:off + (t + 1) * FF_TILE])
        return run

    def stage_jobs(slot):
        return [stage_job(dst_ref, off, t, slot)
                for dst_ref, off in ((fz_ref, OFF_F), (qz_ref, OFF_Q), (vz_ref, OFF_I))
                for t in range(HGRN_W // FF_TILE)]

    def gates_job(ci):
        def run():
            rows = slice(ci * CHUNK, (ci + 1) * CHUNK)
            f = fz_ref[rows, :]
            l1p = jnp.log(1.0 + jnp.exp(-jnp.abs(f)))
            bb = log1m_lb + (jnp.minimum(f, 0.0) - l1p)
            lf2 = (jnp.maximum(log_lb, bb) + jnp.log(1.0 + jnp.exp(-jnp.abs(log_lb - bb)))) * LOG2E
            lk2 = (log1m_lb - jnp.maximum(f, 0.0) - l1p) * LOG2E
            c2 = _sub_scan(lf2)
            lk = lk2 - c2
            qc = qz_ref[rows, :]
            q = qc * _sigmoid(qc)
            vc = vz_ref[rows, :]
            for hd in range(HEADS):
                c2_ref[hd, rows, :] = c2[:, _head(hd)]
                lk_ref[hd, rows, :] = lk[:, _head(hd)]
                q_ref[hd, rows, :] = q[:, _head(hd)]
                v_ref[hd, rows, :] = vc[:, _head(hd)]
            tot = [c2[SUB * j + SUB - 1:SUB * j + SUB, :] for j in range(N_SUB)]
            p1 = tot[0]
            p2 = p1 + tot[1]
            p3 = p2 + tot[2]
            dec_ref[ci] = jnp.exp2(jnp.concatenate(
                [tot[1], tot[2], tot[3], p1, p2, p3, p3 + tot[3], p3 + tot[3]], axis=0))
        return run

    @pl.when(tb == 0)
    def _():
        st_ref[...] = jnp.zeros_like(st_ref)
        ext_ref[0:POOL_CARRY, :] = jnp.zeros((POOL_CARRY, POOL_WIDTH), F32)
        for job in [norm_job(x_ref, 0)] + stage_jobs(0) + [gates_job(ci) for ci in range(n_chunks)]:
            job()

    @pl.when(tb > 0)
    def _():
        h_ref[0] = h_ref[1]

    def proj(col0):
        return _dot(h_ref[0], win_ref[:, col0:col0 + FF_TILE])

    def pool_proj_job(t):
        def run():
            ext_ref[POOL_CARRY:POOL_CARRY + TM_MIX, tile(t)] = proj(OFF_POOL + t * FF_TILE)
        return run

    def og_job(t):
        def run():
            og = proj(OFF_OG + t * FF_TILE)
            og_ref[:, tile(t)] = og * _sigmoid(og)
        return run

    def gate_job(dst_ref, off, t, act):
        def run():
            dst_ref[:, tile(t)] = act(proj(off + t * FF_TILE))
        return run

    def pool_mix_job(gi):
        def run():
            w = POOL_WINDOWS[gi]
            cols = slice(gi * POOL_GROUP_DIM, (gi + 1) * POOL_GROUP_DIM)
            pos = lax.broadcasted_iota(jnp.int32, (POOL_CARRY, POOL_GROUP_DIM), 0) + (tb * TM_MIX + 1)
            ext = ext_ref[:, cols]
            wsum = ext
            d = 1
            while d < w:
                wsum = wsum + pltpu.roll(wsum, d, axis=0)
                d *= 2
            wsum = wsum[POOL_CARRY:, :]
            u = ext[POOL_CARRY:, :]
            inv_head = 1.0 / jnp.minimum(pos, w).astype(F32)
            mean = jnp.concatenate([wsum[:POOL_CARRY] * inv_head, wsum[POOL_CARRY:] * (1.0 / w)], axis=0)
            mixed = _dot((mean - u).astype(BF16), poolw_ref[gi].astype(BF16))
            pm_ref[:, cols] = (mixed * pscale_ref[:, cols]).astype(BF16)
        return run

    def pool_merge_job():
        ga_ref[...] = ga_ref[...] * _dot(pm_ref[...], wpp_ref[...])
        ext_ref[0:POOL_CARRY, :] = ext_ref[TM_MIX:TM_MIX + POOL_CARRY, :]

    og_jobs = [og_job(t) for t in range(HGRN_W // FF_TILE)]
    pool_proj_jobs = [pool_proj_job(t) for t in range(POOL_WIDTH // FF_TILE)]
    ga_jobs = [gate_job(ga_ref, OFF_GA, t, _sigmoid) for t in range(D_MODEL // FF_TILE)]
    gb_jobs = [gate_job(gb_ref, OFF_GB, t, lambda z: z) for t in range(D_MODEL // FF_TILE)]
    mix_jobs = [pool_mix_job(gi) for gi in range(len(POOL_WINDOWS))]
    nxt = stage_jobs(1)
    schedule = [
        og_jobs + [norm_job(xn_ref, 1)],
        nxt[0:3],
        nxt[3:6],
        pool_proj_jobs + [gates_job(0)],
        ga_jobs[0:2] + [gates_job(1), gates_job(2)],
        ga_jobs[2:4] + mix_jobs[0:2] + [gates_job(3)],
        gb_jobs[0:2] + mix_jobs[2:4] + [pool_merge_job, gates_job(4)],
        gb_jobs[2:4] + [gates_job(5)],
    ]
    assert len(schedule) == n_chunks
    gates_after_loop = (n_chunks - 2, n_chunks - 1)

    for ci in range(n_chunks):
        for job in schedule[ci]:
            job()
        r0 = ci * CHUNK
        dec = dec_ref[ci]
        e_tot = [None] + [dec[j - 1:j, :] for j in range(1, N_SUB)]
        e_pre = [None] + [dec[N_SUB - 2 + j:N_SUB - 1 + j, :] for j in range(1, N_SUB)]
        decay = dec[2 * N_SUB - 2:2 * N_SUB - 1, :]
        outs = [_hgrn_chunk_head(r0, hd, c2_ref, lk_ref, q_ref, v_ref, st_ref, e_tot, e_pre, decay)
                for hd in range(HEADS)]
        normed = []
        for oh in outs:
            ms = jnp.mean(oh * oh, axis=-1, keepdims=True)
            normed.append(oh * lax.rsqrt(ms + EPS))
        on = jnp.concatenate(normed, axis=-1) * hnorm_ref[...]
        hg_ref[r0:r0 + CHUNK, :] = (on * og_ref[r0:r0 + CHUNK, :]).astype(BF16)

    for k in gates_after_loop:
        gates_job(k)()
    pb = _dot(hg_ref[...], whp_ref[...])
    merged = (ga_ref[...] + _sigmoid(gb_ref[...]) * pb).astype(BF16)
    o_ref[...] = x_ref[...] + _dot(merged, wout_ref[...])


def _mixer(x, g, win, poolw, pscale, lbl, hnorm, wpp, whp, wout, layer, batch):
    n = x.shape[0]
    nt = n // batch // TM_MIX
    depth = lbl.shape[0]
    hbm = pl.BlockSpec(memory_space=pl.ANY)
    return pl.pallas_call(
        functools.partial(_mixer_kernel, layer=layer),
        grid=(batch, nt),
        in_specs=[
            pl.BlockSpec((TM_MIX, D_MODEL), lambda b, t: (b * nt + t, 0)),
            pl.BlockSpec((TM_MIX, D_MODEL), lambda b, t: (b * nt + jnp.minimum(t + 1, nt - 1), 0)),
            _const_spec((1, D_MODEL)),
            hbm,
            _const_spec((len(POOL_WINDOWS), POOL_GROUP_DIM, POOL_GROUP_DIM)),
            _const_spec((1, POOL_WIDTH)),
            _const_spec((depth, HGRN_W)),
            _const_spec((1, HGRN_W)),
            hbm, hbm, hbm,
        ],
        out_specs=pl.BlockSpec((TM_MIX, D_MODEL), lambda b, t: (b * nt + t, 0)),
        out_shape=jax.ShapeDtypeStruct((n, D_MODEL), F32),
        scratch_shapes=[
            pltpu.VMEM((2, TM_MIX, D_MODEL), BF16),
            pltpu.VMEM((TM_MIX, HGRN_W), F32),
            pltpu.VMEM((TM_MIX, HGRN_W), F32),
            pltpu.VMEM((TM_MIX, HGRN_W), F32),
            pltpu.VMEM((HEADS, TM_MIX, HEAD_DIM), F32),
            pltpu.VMEM((HEADS, TM_MIX, HEAD_DIM), F32),
            pltpu.VMEM((HEADS, TM_MIX, HEAD_DIM), F32),
            pltpu.VMEM((HEADS, TM_MIX, HEAD_DIM), F32),
            pltpu.VMEM((TM_MIX // CHUNK, 8, HGRN_W), F32),
            pltpu.VMEM((POOL_CARRY + TM_MIX, POOL_WIDTH), F32),
            pltpu.VMEM((TM_MIX, HGRN_W), BF16),
            pltpu.VMEM((TM_MIX, POOL_WIDTH), BF16),
            pltpu.VMEM((TM_MIX, HGRN_W), F32),
            pltpu.VMEM((TM_MIX, D_MODEL), F32),
            pltpu.VMEM((TM_MIX, D_MODEL), F32),
            pltpu.VMEM((HEADS, HEAD_DIM, HEAD_DIM), F32),
            pltpu.VMEM((D_MODEL, D_IN), BF16),
            pltpu.VMEM((POOL_WIDTH, D_MODEL), BF16),
            pltpu.VMEM((HGRN_W, D_MODEL), BF16),
            pltpu.VMEM((D_MODEL, D_MODEL), BF16),
            pltpu.VMEM((2, CAST_ROWS_IN_PROJ, D_IN), F32),
            pltpu.VMEM((2, CAST_ROWS_SQUARE, D_MODEL), F32),
            pltpu.SemaphoreType.DMA((2,)),
        ],
        compiler_params=pltpu.CompilerParams(
            dimension_semantics=("arbitrary", "arbitrary"), vmem_limit_bytes=VMEM_LIMIT_BYTES),
        name="hybrid_mixer",
    )(x, x, g, win, poolw, pscale, lbl, hnorm, wpp, whp, wout)


def kernel(x, ffn1_norm, ffn1_w_gate, ffn1_w_up, ffn1_w_down, mix_norm, w_in, pool_w, pool_scale,
           lb_logits, hgrn_norm, w_pool_proj, w_hgrn_proj, w_out, ffn2_norm, ffn2_w_gate, ffn2_w_up,
           ffn2_w_down, final_norm):
    batch, seq, _ = x.shape
    depth = ffn1_norm.shape[0]
    xf = x.reshape(batch * seq, D_MODEL)
    fg = final_norm.reshape(1, D_MODEL)
    for l in range(depth):
        xf = _ffn(xf, ffn1_norm[l].reshape(1, D_MODEL), ffn1_w_gate, ffn1_w_up, ffn1_w_down, fg, l, False)
        xf = _mixer(xf, mix_norm[l].reshape(1, D_MODEL), w_in, pool_w[l],
                    pool_scale[l].reshape(1, POOL_WIDTH), lb_logits, hgrn_norm[l].reshape(1, HGRN_W),
                    w_pool_proj, w_hgrn_proj, w_out, l, batch)
        xf = _ffn(xf, ffn2_norm[l].reshape(1, D_MODEL), ffn2_w_gate, ffn2_w_up, ffn2_w_down, fg, l,
                  l == depth - 1)
    return xf.reshape(batch, seq, D_MODEL)
```

```python
import functools

import jax
import jax.numpy as jnp
from jax import lax
from jax.experimental import pallas as pl
from jax.experimental.pallas import tpu as pltpu

D_MODEL = 1024
D_FF = 2816
CHUNK = 64
SUB = 16
N_SUB = CHUNK // SUB
POOL_WIDTH = 512
POOL_WINDOWS = (2, 4, 8, 16)
POOL_GROUP_DIM = 128
POOL_CARRY = 16
HEADS = 4
HEAD_DIM = 128
HGRN_W = HEADS * HEAD_DIM
EPS = 1e-6
LOG2E = 1.4426950408889634

FF_TILE = 256
N_FF_TILES = D_FF // FF_TILE
TM_FFN = 512
TM_MIX = 512
CAST_ROWS_FF_IN = 128
CAST_ROWS_FF_OUT = 352
CAST_ROWS_IN_PROJ = 64
CAST_ROWS_SQUARE = 256
VMEM_LIMIT_BYTES = 56 * 1024 * 1024

OFF_POOL = 0
OFF_Q = POOL_WIDTH
OFF_F = OFF_Q + HGRN_W
OFF_I = OFF_F + HGRN_W
OFF_OG = OFF_I + HGRN_W
OFF_GA = OFF_OG + HGRN_W
OFF_GB = OFF_GA + D_MODEL
D_IN = OFF_GB + D_MODEL

F32 = jnp.float32
BF16 = jnp.bfloat16


def _dot(a, b):
    return jnp.dot(a, b, preferred_element_type=F32)


def _dot_nt(a, b):
    return lax.dot_general(a, b, (((1,), (1,)), ((), ())), preferred_element_type=F32)


def _dot_tn(a, b):
    return lax.dot_general(a, b, (((0,), (0,)), ((), ())), preferred_element_type=F32)


def _rmsnorm(x, g):
    ms = jnp.mean(x * x, axis=-1, keepdims=True)
    return x * lax.rsqrt(ms + EPS) * g


def _sigmoid(x):
    return 1.0 / (1.0 + jnp.exp(-x))


def _weight_chunk_copy(src_hbm, layer, stage_ref, sem_ref, c):
    rows = stage_ref.shape[1]
    return pltpu.make_async_copy(src_hbm.at[layer, pl.ds(c * rows, rows), :], stage_ref.at[c % 2],
                                 sem_ref.at[c % 2])


def _cast_weight(src_hbm, layer, dst_ref, stage_ref, sem_ref):
    rows = stage_ref.shape[1]
    n_chunks, rem = divmod(dst_ref.shape[0], rows)
    assert rem == 0
    _weight_chunk_copy(src_hbm, layer, stage_ref, sem_ref, 0).start()
    for c in range(n_chunks):
        if c + 1 < n_chunks:
            _weight_chunk_copy(src_hbm, layer, stage_ref, sem_ref, c + 1).start()
        _weight_chunk_copy(src_hbm, layer, stage_ref, sem_ref, c).wait()
        dst_ref[c * rows:(c + 1) * rows, :] = stage_ref[c % 2].astype(BF16)


def _ffn_kernel(x_ref, g_ref, wg_hbm, wu_hbm, wd_hbm, fg_ref, o_ref, h_ref, a_ref, wg_ref, wu_ref, wd_ref,
                stage_in_ref, stage_out_ref, sem_ref, *, layer, final_norm):
    @pl.when(pl.program_id(0) == 0)
    def _():
        _cast_weight(wg_hbm, layer, wg_ref, stage_in_ref, sem_ref)
        _cast_weight(wu_hbm, layer, wu_ref, stage_in_ref, sem_ref)
        _cast_weight(wd_hbm, layer, wd_ref, stage_out_ref, sem_ref)

    x = x_ref[...]
    h_ref[...] = _rmsnorm(x, g_ref[...]).astype(BF16)
    for j in range(N_FF_TILES):
        cols = slice(j * FF_TILE, (j + 1) * FF_TILE)
        h = h_ref[...]
        g = _dot(h, wg_ref[:, cols])
        u = _dot(h, wu_ref[:, cols])
        a_ref[:, cols] = (g * _sigmoid(g) * u).astype(BF16)
    y = x + 0.5 * _dot(a_ref[...], wd_ref[...])
    if final_norm:
        y = _rmsnorm(y, fg_ref[...])
    o_ref[...] = y


def _const_spec(shape):
    nd = len(shape)
    return pl.BlockSpec(shape, lambda *_: (0,) * nd, pipeline_mode=pl.Buffered(1))


def _ffn(x, g, wg, wu, wd, fg, layer, final_norm):
    n = x.shape[0]
    hbm = pl.BlockSpec(memory_space=pl.ANY)
    return pl.pallas_call(
        functools.partial(_ffn_kernel, layer=layer, final_norm=final_norm),
        grid=(n // TM_FFN,),
        in_specs=[
            pl.BlockSpec((TM_FFN, D_MODEL), lambda i: (i, 0)),
            _const_spec((1, D_MODEL)),
            hbm, hbm, hbm,
            _const_spec((1, D_MODEL)),
        ],
        out_specs=pl.BlockSpec((TM_FFN, D_MODEL), lambda i: (i, 0)),
        out_shape=jax.ShapeDtypeStruct((n, D_MODEL), F32),
        scratch_shapes=[
            pltpu.VMEM((TM_FFN, D_MODEL), BF16),
            pltpu.VMEM((TM_FFN, D_FF), BF16),
            pltpu.VMEM((D_MODEL, D_FF), BF16),
            pltpu.VMEM((D_MODEL, D_FF), BF16),
            pltpu.VMEM((D_FF, D_MODEL), BF16),
            pltpu.VMEM((2, CAST_ROWS_FF_IN, D_FF), F32),
            pltpu.VMEM((2, CAST_ROWS_FF_OUT, D_MODEL), F32),
            pltpu.SemaphoreType.DMA((2,)),
        ],
        compiler_params=pltpu.CompilerParams(
            dimension_semantics=("arbitrary",), vmem_limit_bytes=VMEM_LIMIT_BYTES),
        name="swiglu_half_step",
    )(x, g, wg, wu, wd, fg)


def _lower_bound(lbl_ref, layer):
    logits = lbl_ref[...]
    m = jnp.max(logits, axis=0, keepdims=True)
    e = jnp.exp(logits - m)
    p = e / jnp.sum(e, axis=0, keepdims=True)
    lb = jnp.zeros((1, HGRN_W), F32)
    for j in range(1, layer + 1):
        lb = lb + p[j:j + 1, :]
    return lb


def _head(hd):
    return slice(hd * HEAD_DIM, (hd + 1) * HEAD_DIM)


def _sub_scan(lf2):
    n8 = CHUNK // 8
    w = lf2.reshape(n8, 8, HGRN_W)
    sub8 = lax.broadcasted_iota(jnp.int32, (n8, 8, HGRN_W), 1)
    for d in (1, 2, 4):
        w = w + jnp.where(sub8 >= d, pltpu.roll(w, d, axis=1), 0.0)
    w = w.reshape(CHUNK, HGRN_W)
    parts = []
    for j in range(N_SUB):
        lo = w[SUB * j:SUB * j + 8, :]
        hi = w[SUB * j + 8:SUB * j + 16, :] + lo[7:8, :]
        parts += [lo, hi]
    return jnp.concatenate(parts, axis=0)


def _hgrn_chunk_head(r0, hd, c2_ref, lk_ref, q_ref, v_ref, st_ref, e_tot, e_pre, decay):
    hs = _head(hd)
    rows = slice(r0, r0 + CHUNK)
    c2 = c2_ref[hd, rows, :]
    lk = lk_ref[hd, rows, :]
    q = q_ref[hd, rows, :]
    v_b = v_ref[hd, rows, :].astype(BF16)

    def sub(a, j):
        return a[SUB * j:SUB * (j + 1), :]

    tot = [c2[SUB * j + SUB - 1:SUB * j + SUB, :] for j in range(N_SUB)]
    tot_own = jnp.concatenate([jnp.broadcast_to(t, (SUB, HEAD_DIM)) for t in tot], axis=0)
    qh = q * jnp.exp2(c2)
    kh = jnp.exp2(lk + tot_own)
    et = [None] + [e_tot[j][:, hs] for j in range(1, N_SUB)]
    qg = jnp.concatenate([sub(qh, 0)] + [sub(qh, j) * e_pre[j][:, hs] for j in range(1, N_SUB)],
                         axis=0).astype(BF16)

    k2_0 = sub(kh, 0) * et[1]
    k3_0 = k2_0 * et[2]
    k3_1 = sub(kh, 1) * et[2]
    kg = jnp.concatenate([k3_0 * et[3], k3_1 * et[3], sub(kh, 2) * et[3], sub(kh, 3)],
                         axis=0).astype(BF16)
    earlier = [[], [sub(kh, 0)], [k2_0, sub(kh, 1)], [k3_0, k3_1, sub(kh, 2)]]

    half = SUB // 2
    zero8 = jnp.zeros((half, HEAD_DIM), F32)
    lane = lax.broadcasted_iota(jnp.int32, (half, CHUNK), 1)
    row8 = lax.broadcasted_iota(jnp.int32, (half, CHUNK), 0)
    blocks = []
    for j in range(N_SUB):
        base = r0 + SUB * j
        c_lo, c_hi = c2[SUB * j:SUB * j + half, :], c2[SUB * j + half:SUB * (j + 1), :]
        q_lo, q_hi = q[SUB * j:SUB * j + half, :], q[SUB * j + half:SUB * (j + 1), :]
        c_mid = c2[SUB * j + half - 1:SUB * j + half, :]
        q_mid = q_hi * jnp.exp2(c_hi - c_mid)
        k_mid = jnp.exp2(lk[SUB * j:SUB * j + half, :] + c_mid)
        lhs = jnp.concatenate([sub(qh, j), q_mid, zero8], axis=0).astype(BF16)
        rhs = jnp.concatenate(earlier[j] + [k_mid] + [zero8] * (2 * (N_SUB - j) - 1), axis=0).astype(BF16)
        r = _dot_nt(lhs, rhs)

        d_lo = jnp.zeros((half, CHUNK), F32)
        d_hi = jnp.zeros((half, CHUNK), F32)
        for s in range(half):
            lks = jnp.broadcast_to(lk_ref[hd, base + s:base + s + 1, :], (half, HEAD_DIM))
            col_lo = jnp.sum(q_lo * jnp.exp2(c_lo + lks), axis=-1, keepdims=True)
            d_lo = jnp.where(lane == SUB * j + s, col_lo, d_lo)
            lks = jnp.broadcast_to(lk_ref[hd, base + half + s:base + half + s + 1, :], (half, HEAD_DIM))
            col_hi = jnp.sum(q_hi * jnp.exp2(c_hi + lks), axis=-1, keepdims=True)
            d_hi = jnp.where(lane == SUB * j + half + s, col_hi, d_hi)
        d_lo = jnp.where(lane <= SUB * j + row8, d_lo, 0.0)
        d_hi = jnp.where(lane <= SUB * j + half + row8, d_hi, 0.0)
        a_lo = jnp.where(lane < SUB * j, r[0:half], d_lo)
        a_hi = jnp.where(lane < SUB * j, r[half:SUB], jnp.where(lane < SUB * j + half, r[SUB:SUB + half], d_hi))
        blocks += [a_lo, a_hi]
    a = jnp.concatenate(blocks, axis=0).astype(BF16)
    o_intra = _dot(a, v_b)

    st = st_ref[hd]
    o_state = _dot_nt(qg, st.astype(BF16))
    st_ref[hd] = decay[:, hs] * st + _dot_tn(v_b, kg)
    return o_state + o_intra


def _mixer_kernel(x_ref, xn_ref, g_ref, win_hbm, poolw_ref, pscale_ref, lbl_ref, hnorm_ref, wpp_hbm, whp_hbm,
                  wout_hbm, o_ref, h_ref, fz_ref, qz_ref, vz_ref, c2_ref, lk_ref, q_ref, v_ref, dec_ref, ext_ref,
                  hg_ref, pm_ref, og_ref, ga_ref, gb_ref, st_ref, win_ref, wpp_ref, whp_ref, wout_ref,
                  stage_in_ref, stage_sq_ref, sem_ref, *, layer):
    tb = pl.program_id(1)
    n_chunks = TM_MIX // CHUNK

    @pl.when((pl.program_id(0) == 0) & (tb == 0))
    def _():
        _cast_weight(win_hbm, layer, win_ref, stage_in_ref, sem_ref)
        _cast_weight(wpp_hbm, layer, wpp_ref, stage_sq_ref, sem_ref)
        _cast_weight(whp_hbm, layer, whp_ref, stage_sq_ref, sem_ref)
        _cast_weight(wout_hbm, layer, wout_ref, stage_sq_ref, sem_ref)

    lb = _lower_bound(lbl_ref, layer)
    log_lb = jnp.log(lb)
    log1m_lb = jnp.log1p(-lb)

    def tile(t):
        return slice(t * FF_TILE, (t + 1) * FF_TILE)

    def norm_job(src_ref, slot):
        def run():
            h_ref[slot] = _rmsnorm(src_ref[...], g_ref[...]).astype(BF16)
        return run

    def stage_job(dst_ref, off, t, slot):
        def run():
            dst_ref[:, tile(t)] = _dot(h_ref[slot], win_ref[:, off + t * FF_TILE:off + (t + 1) * FF_TILE])
        return run

    def stage_jobs(slot):
        return [stage_job(dst_ref, off, t, slot)
                for dst_ref, off in ((fz_ref, OFF_F), (qz_ref, OFF_Q), (vz_ref, OFF_I))
                for t in range(HGRN_W // FF_TILE)]

    def gates_job(ci):
        def run():
            rows = slice(ci * CHUNK, (ci + 1) * CHUNK)
            f = fz_ref[rows, :]
            l1p = jnp.log(1.0 + jnp.exp(-jnp.abs(f)))
            bb = log1m_lb + (jnp.minimum(f, 0.0) - l1p)
            lf2 = (jnp.maximum(log_lb, bb) + jnp.log(1.0 + jnp.exp(-jnp.abs(log_lb - bb)))) * LOG2E
            lk2 = (log1m_lb - jnp.maximum(f, 0.0) - l1p) * LOG2E
            c2 = _sub_scan(lf2)
            lk = lk2 - c2
            qc = qz_ref[rows, :]
            q = qc * _sigmoid(qc)
            vc = vz_ref[rows, :]
            for hd in range(HEADS):
                c2_ref[hd, rows, :] = c2[:, _head(hd)]
                lk_ref[hd, rows, :] = lk[:, _head(hd)]
                q_ref[hd, rows, :] = q[:, _head(hd)]
                v_ref[hd, rows, :] = vc[:, _head(hd)]
            tot = [c2[SUB * j + SUB - 1:SUB * j + SUB, :] for j in range(N_SUB)]
            p1 = tot[0]
            p2 = p1 + tot[1]
            p3 = p2 + tot[2]
            dec_ref[ci] = jnp.exp2(jnp.concatenate(
                [tot[1], tot[2], tot[3], p1, p2, p3, p3 + tot[3], p3 + tot[3]], axis=0))
        return run

    @pl.when(tb == 0)
    def _():
        st_ref[...] = jnp.zeros_like(st_ref)
        ext_ref[0:POOL_CARRY, :] = jnp.zeros((POOL_CARRY, POOL_WIDTH), F32)
        for job in [norm_job(x_ref, 0)] + stage_jobs(0) + [gates_job(ci) for ci in range(n_chunks)]:
            job()

    @pl.when(tb > 0)
    def _():
        h_ref[0] = h_ref[1]

    def proj(col0):
        return _dot(h_ref[0], win_ref[:, col0:col0 + FF_TILE])

    def pool_proj_job(t):
        def run():
            ext_ref[POOL_CARRY:POOL_CARRY + TM_MIX, tile(t)] = proj(OFF_POOL + t * FF_TILE)
        return run

    def og_job(t):
        def run():
            og = proj(OFF_OG + t * FF_TILE)
            og_ref[:, tile(t)] = og * _sigmoid(og)
        return run

    def gate_job(dst_ref, off, t, act):
        def run():
            dst_ref[:, tile(t)] = act(proj(off + t * FF_TILE))
        return run

    def pool_mix_job(gi):
        def run():
            w = POOL_WINDOWS[gi]
            cols = slice(gi * POOL_GROUP_DIM, (gi + 1) * POOL_GROUP_DIM)
            pos = lax.broadcasted_iota(jnp.int32, (POOL_CARRY, POOL_GROUP_DIM), 0) + (tb * TM_MIX + 1)
            ext = ext_ref[:, cols]
            wsum = ext
            d = 1
            while d < w:
                wsum = wsum + pltpu.roll(wsum, d, axis=0)
                d *= 2
            wsum = wsum[POOL_CARRY:, :]
            u = ext[POOL_CARRY:, :]
            inv_head = 1.0 / jnp.minimum(pos, w).astype(F32)
            mean = jnp.concatenate([wsum[:POOL_CARRY] * inv_head, wsum[POOL_CARRY:] * (1.0 / w)], axis=0)
            mixed = _dot((mean - u).astype(BF16), poolw_ref[gi].astype(BF16))
            pm_ref[:, cols] = (mixed * pscale_ref[:, cols]).astype(BF16)
        return run

    def pool_merge_job():
        ga_ref[...] = ga_ref[...] * _dot(pm_ref[...], wpp_ref[...])
        ext_ref[0:POOL_CARRY, :] = ext_ref[TM_MIX:TM_MIX + POOL_CARRY, :]

    og_jobs = [og_job(t) for t in range(HGRN_W // FF_TILE)]
    pool_proj_jobs = [pool_proj_job(t) for t in range(POOL_WIDTH // FF_TILE)]
    ga_jobs = [gate_job(ga_ref, OFF_GA, t, _sigmoid) for t in range(D_MODEL // FF_TILE)]
    gb_jobs = [gate_job(gb_ref, OFF_GB, t, lambda z: z) for t in range(D_MODEL // FF_TILE)]
    mix_jobs = [pool_mix_job(gi) for gi in range(len(POOL_WINDOWS))]
    nxt = stage_jobs(1)
    schedule = [
        og_jobs + [norm_job(xn_ref, 1)],
        nxt[0:3],
        nxt[3:6],
        pool_proj_jobs + [gates_job(0)],
        ga_jobs[0:2] + [gates_job(1), gates_job(2)],
        ga_jobs[2:4] + mix_jobs[0:2] + [gates_job(3)],
        gb_jobs[0:2] + mix_jobs[2:4] + [pool_merge_job, gates_job(4)],
        gb_jobs[2:4] + [gates_job(5)],
    ]
    assert len(schedule) == n_chunks
    gates_after_loop = (n_chunks - 2, n_chunks - 1)

    for ci in range(n_chunks):
        for job in schedule[ci]:
            job()
        r0 = ci * CHUNK
        dec = dec_ref[ci]
        e_tot = [None] + [dec[j - 1:j, :] for j in range(1, N_SUB)]
        e_pre = [None] + [dec[N_SUB - 2 + j:N_SUB - 1 + j, :] for j in range(1, N_SUB)]
        decay = dec[2 * N_SUB - 2:2 * N_SUB - 1, :]
        outs = [_hgrn_chunk_head(r0, hd, c2_ref, lk_ref, q_ref, v_ref, st_ref, e_tot, e_pre, decay)
                for hd in range(HEADS)]
        normed = []
        for oh in outs:
            ms = jnp.mean(oh * oh, axis=-1, keepdims=True)
            normed.append(oh * lax.rsqrt(ms + EPS))
        on = jnp.concatenate(normed, axis=-1) * hnorm_ref[...]
        hg_ref[r0:r0 + CHUNK, :] = (on * og_ref[r0:r0 + CHUNK, :]).astype(BF16)

    for k in gates_after_loop:
        gates_job(k)()
    pb = _dot(hg_ref[...], whp_ref[...])
    merged = (ga_ref[...] + _sigmoid(gb_ref[...]) * pb).astype(BF16)
    o_ref[...] = x_ref[...] + _dot(merged, wout_ref[...])


def _mixer(x, g, win, poolw, pscale, lbl, hnorm, wpp, whp, wout, layer, batch):
    n = x.shape[0]
    nt = n // batch // TM_MIX
    depth = lbl.shape[0]
    hbm = pl.BlockSpec(memory_space=pl.ANY)
    return pl.pallas_call(
        functools.partial(_mixer_kernel, layer=layer),
        grid=(batch, nt),
        in_specs=[
            pl.BlockSpec((TM_MIX, D_MODEL), lambda b, t: (b * nt + t, 0)),
            pl.BlockSpec((TM_MIX, D_MODEL), lambda b, t: (b * nt + jnp.minimum(t + 1, nt - 1), 0)),
            _const_spec((1, D_MODEL)),
            hbm,
            _const_spec((len(POOL_WINDOWS), POOL_GROUP_DIM, POOL_GROUP_DIM)),
            _const_spec((1, POOL_WIDTH)),
            _const_spec((depth, HGRN_W)),
            _const_spec((1, HGRN_W)),
            hbm, hbm, hbm,
        ],
        out_specs=pl.BlockSpec((TM_MIX, D_MODEL), lambda b, t: (b * nt + t, 0)),
        out_shape=jax.ShapeDtypeStruct((n, D_MODEL), F32),
        scratch_shapes=[
            pltpu.VMEM((2, TM_MIX, D_MODEL), BF16),
            pltpu.VMEM((TM_MIX, HGRN_W), F32),
            pltpu.VMEM((TM_MIX, HGRN_W), F32),
            pltpu.VMEM((TM_MIX, HGRN_W), F32),
            pltpu.VMEM((HEADS, TM_MIX, HEAD_DIM), F32),
            pltpu.VMEM((HEADS, TM_MIX, HEAD_DIM), F32),
            pltpu.VMEM((HEADS, TM_MIX, HEAD_DIM), F32),
            pltpu.VMEM((HEADS, TM_MIX, HEAD_DIM), F32),
            pltpu.VMEM((TM_MIX // CHUNK, 8, HGRN_W), F32),
            pltpu.VMEM((POOL_CARRY + TM_MIX, POOL_WIDTH), F32),
            pltpu.VMEM((TM_MIX, HGRN_W), BF16),
            pltpu.VMEM((TM_MIX, POOL_WIDTH), BF16),
            pltpu.VMEM((TM_MIX, HGRN_W), F32),
            pltpu.VMEM((TM_MIX, D_MODEL), F32),
            pltpu.VMEM((TM_MIX, D_MODEL), F32),
            pltpu.VMEM((HEADS, HEAD_DIM, HEAD_DIM), F32),
            pltpu.VMEM((D_MODEL, D_IN), BF16),
            pltpu.VMEM((POOL_WIDTH, D_MODEL), BF16),
            pltpu.VMEM((HGRN_W, D_MODEL), BF16),
            pltpu.VMEM((D_MODEL, D_MODEL), BF16),
            pltpu.VMEM((2, CAST_ROWS_IN_PROJ, D_IN), F32),
            pltpu.VMEM((2, CAST_ROWS_SQUARE, D_MODEL), F32),
            pltpu.SemaphoreType.DMA((2,)),
        ],
        compiler_params=pltpu.CompilerParams(
            dimension_semantics=("arbitrary", "arbitrary"), vmem_limit_bytes=VMEM_LIMIT_BYTES),
        name="hybrid_mixer",
    )(x, x, g, win, poolw, pscale, lbl, hnorm, wpp, whp, wout)


def kernel(x, ffn1_norm, ffn1_w_gate, ffn1_w_up, ffn1_w_down, mix_norm, w_in, pool_w, pool_scale,
           lb_logits, hgrn_norm, w_pool_proj, w_hgrn_proj, w_out, ffn2_norm, ffn2_w_gate, ffn2_w_up,
           ffn2_w_down, final_norm):
    batch, seq, _ = x.shape
    depth = ffn1_norm.shape[0]
    xf = x.reshape(batch * seq, D_MODEL)
    fg = final_norm.reshape(1, D_MODEL)
    for l in range(depth):
        xf = _ffn(xf, ffn1_norm[l].reshape(1, D_MODEL), ffn1_w_gate, ffn1_w_up, ffn1_w_down, fg, l, False)
        xf = _mixer(xf, mix_norm[l].reshape(1, D_MODEL), w_in, pool_w[l],
                    pool_scale[l].reshape(1, POOL_WIDTH), lb_logits, hgrn_norm[l].reshape(1, HGRN_W),
                    w_pool_proj, w_hgrn_proj, w_out, l, batch)
        xf = _ffn(xf, ffn2_norm[l].reshape(1, D_MODEL), ffn2_w_gate, ffn2_w_up, ffn2_w_down, fg, l,
                  l == depth - 1)
    return xf.reshape(batch, seq, D_MODEL)
```

```python
import functools

import jax
import jax.numpy as jnp
from jax import lax
from jax.experimental import pallas as pl
from jax.experimental.pallas import tpu as pltpu

D_MODEL = 1024
D_FF = 2816
CHUNK = 64
SUB = 16
N_SUB = CHUNK // SUB
POOL_WIDTH = 512
POOL_WINDOWS = (2, 4, 8, 16)
POOL_GROUP_DIM = 128
POOL_CARRY = 16
HEADS = 4
HEAD_DIM = 128
HGRN_W = HEADS * HEAD_DIM
EPS = 1e-6
LOG2E = 1.4426950408889634

FF_TILE = 256
N_FF_TILES = D_FF // FF_TILE
TM_FFN = 512
TM_MIX = 512
CAST_ROWS_IN_PROJ = 64
CAST_ROWS_SQUARE = 256
VMEM_LIMIT_BYTES = 56 * 1024 * 1024

OFF_POOL = 0
OFF_Q = POOL_WIDTH
OFF_F = OFF_Q + HGRN_W
OFF_I = OFF_F + HGRN_W
OFF_OG = OFF_I + HGRN_W
OFF_GA = OFF_OG + HGRN_W
OFF_GB = OFF_GA + D_MODEL
D_IN = OFF_GB + D_MODEL

F32 = jnp.float32
BF16 = jnp.bfloat16


def _dot(a, b):
    return jnp.dot(a, b, preferred_element_type=F32)


def _dot_nt(a, b):
    return lax.dot_general(a, b, (((1,), (1,)), ((), ())), preferred_element_type=F32)


def _dot_tn(a, b):
    return lax.dot_general(a, b, (((0,), (0,)), ((), ())), preferred_element_type=F32)


def _rmsnorm(x, g):
    ms = jnp.mean(x * x, axis=-1, keepdims=True)
    return x * lax.rsqrt(ms + EPS) * g


def _sigmoid(x):
    return 1.0 / (1.0 + jnp.exp(-x))


def _weight_chunk_copy(src_hbm, layer, stage_ref, sem_ref, c):
    rows = stage_ref.shape[1]
    return pltpu.make_async_copy(src_hbm.at[layer, pl.ds(c * rows, rows), :], stage_ref.at[c % 2],
                                 sem_ref.at[c % 2])


def _cast_weight(src_hbm, layer, dst_ref, stage_ref, sem_ref):
    rows = stage_ref.shape[1]
    n_chunks, rem = divmod(dst_ref.shape[0], rows)
    assert rem == 0
    _weight_chunk_copy(src_hbm, layer, stage_ref, sem_ref, 0).start()
    for c in range(n_chunks):
        if c + 1 < n_chunks:
            _weight_chunk_copy(src_hbm, layer, stage_ref, sem_ref, c + 1).start()
        _weight_chunk_copy(src_hbm, layer, stage_ref, sem_ref, c).wait()
        dst_ref[c * rows:(c + 1) * rows, :] = stage_ref[c % 2].astype(BF16)


def _ffn_kernel(x_ref, g_ref, wg_hbm, wu_hbm, wd_hbm, fg_ref, o_ref, h_ref, a_ref, wg_ref, wu_ref, wd_ref,
                stage_in_ref, stage_out_ref, sem_ref, *, layer, final_norm):
    def tile_copies(j):
        k = j % 2
        cols = pl.ds(j * FF_TILE, FF_TILE)
        return (pltpu.make_async_copy(wg_hbm.at[layer, :, cols], stage_in_ref.at[0, k], sem_ref.at[0, k]),
                pltpu.make_async_copy(wu_hbm.at[layer, :, cols], stage_in_ref.at[1, k], sem_ref.at[1, k]),
                pltpu.make_async_copy(wd_hbm.at[layer, cols, :], stage_out_ref.at[k], sem_ref.at[2, k]))

    def fetch_tile(j):
        for copy in tile_copies(j):
            copy.start()

    def convert_tile(j):
        cols = slice(j * FF_TILE, (j + 1) * FF_TILE)
        if j + 1 < N_FF_TILES:
            fetch_tile(j + 1)
        for copy in tile_copies(j):
            copy.wait()
        wg_ref[:, cols] = stage_in_ref[0, j % 2].astype(BF16)
        wu_ref[:, cols] = stage_in_ref[1, j % 2].astype(BF16)
        wd_ref[cols, :] = stage_out_ref[j % 2].astype(BF16)

    def half_step(before_tile):
        x = x_ref[...]
        h_ref[...] = _rmsnorm(x, g_ref[...]).astype(BF16)
        for j in range(N_FF_TILES):
            before_tile(j)
            cols = slice(j * FF_TILE, (j + 1) * FF_TILE)
            h = h_ref[...]
            g = _dot(h, wg_ref[:, cols])
            u = _dot(h, wu_ref[:, cols])
            a_ref[:, cols] = (g * _sigmoid(g) * u).astype(BF16)
        y = x + 0.5 * _dot(a_ref[...], wd_ref[...])
        if final_norm:
            y = _rmsnorm(y, fg_ref[...])
        o_ref[...] = y

    @pl.when(pl.program_id(0) == 0)
    def _():
        fetch_tile(0)
        half_step(convert_tile)

    @pl.when(pl.program_id(0) > 0)
    def _():
        half_step(lambda j: None)


def _const_spec(shape):
    nd = len(shape)
    return pl.BlockSpec(shape, lambda *_: (0,) * nd, pipeline_mode=pl.Buffered(1))


def _ffn(x, g, wg, wu, wd, fg, layer, final_norm):
    n = x.shape[0]
    hbm = pl.BlockSpec(memory_space=pl.ANY)
    return pl.pallas_call(
        functools.partial(_ffn_kernel, layer=layer, final_norm=final_norm),
        grid=(n // TM_FFN,),
        in_specs=[
            pl.BlockSpec((TM_FFN, D_MODEL), lambda i: (i, 0)),
            _const_spec((1, D_MODEL)),
            hbm, hbm, hbm,
            _const_spec((1, D_MODEL)),
        ],
        out_specs=pl.BlockSpec((TM_FFN, D_MODEL), lambda i: (i, 0)),
        out_shape=jax.ShapeDtypeStruct((n, D_MODEL), F32),
        scratch_shapes=[
            pltpu.VMEM((TM_FFN, D_MODEL), BF16),
            pltpu.VMEM((TM_FFN, D_FF), BF16),
            pltpu.VMEM((D_MODEL, D_FF), BF16),
            pltpu.VMEM((D_MODEL, D_FF), BF16),
            pltpu.VMEM((D_FF, D_MODEL), BF16),
            pltpu.VMEM((2, 2, D_MODEL, FF_TILE), F32),
            pltpu.VMEM((2, FF_TILE, D_MODEL), F32),
            pltpu.SemaphoreType.DMA((3, 2)),
        ],
        compiler_params=pltpu.CompilerParams(
            dimension_semantics=("arbitrary",), vmem_limit_bytes=VMEM_LIMIT_BYTES),
        name="swiglu_half_step",
    )(x, g, wg, wu, wd, fg)


def _lower_bound(lbl_ref, layer):
    logits = lbl_ref[...]
    m = jnp.max(logits, axis=0, keepdims=True)
    e = jnp.exp(logits - m)
    p = e / jnp.sum(e, axis=0, keepdims=True)
    lb = jnp.zeros((1, HGRN_W), F32)
    for j in range(1, layer + 1):
        lb = lb + p[j:j + 1, :]
    return lb


def _head(hd):
    return slice(hd * HEAD_DIM, (hd + 1) * HEAD_DIM)


def _sub_scan(lf2):
    n8 = CHUNK // 8
    w = lf2.reshape(n8, 8, HGRN_W)
    sub8 = lax.broadcasted_iota(jnp.int32, (n8, 8, HGRN_W), 1)
    for d in (1, 2, 4):
        w = w + jnp.where(sub8 >= d, pltpu.roll(w, d, axis=1), 0.0)
    w = w.reshape(CHUNK, HGRN_W)
    parts = []
    for j in range(N_SUB):
        lo = w[SUB * j:SUB * j + 8, :]
        hi = w[SUB * j + 8:SUB * j + 16, :] + lo[7:8, :]
        parts += [lo, hi]
    return jnp.concatenate(parts, axis=0)


def _hgrn_chunk_head(r0, hd, c2_ref, lk_ref, q_ref, v_ref, st_ref, e_tot, e_pre, decay):
    hs = _head(hd)
    rows = slice(r0, r0 + CHUNK)
    c2 = c2_ref[hd, rows, :]
    lk = lk_ref[hd, rows, :]
    q = q_ref[hd, rows, :]
    v_b = v_ref[hd, rows, :].astype(BF16)

    def sub(a, j):
        return a[SUB * j:SUB * (j + 1), :]

    tot = [c2[SUB * j + SUB - 1:SUB * j + SUB, :] for j in range(N_SUB)]
    tot_own = jnp.concatenate([jnp.broadcast_to(t, (SUB, HEAD_DIM)) for t in tot], axis=0)
    qh = q * jnp.exp2(c2)
    kh = jnp.exp2(lk + tot_own)
    et = [None] + [e_tot[j][:, hs] for j in range(1, N_SUB)]
    qg = jnp.concatenate([sub(qh, 0)] + [sub(qh, j) * e_pre[j][:, hs] for j in range(1, N_SUB)],
                         axis=0).astype(BF16)
    qh_b = qh.astype(BF16)

    zero = jnp.zeros((SUB, HEAD_DIM), F32)
    k2_0 = sub(kh, 0) * et[1]
    k3_0 = k2_0 * et[2]
    k3_1 = sub(kh, 1) * et[2]
    k1 = jnp.concatenate([sub(kh, 0), zero, zero, zero], axis=0).astype(BF16)
    k2 = jnp.concatenate([k2_0, sub(kh, 1), zero, zero], axis=0).astype(BF16)
    k3 = jnp.concatenate([k3_0, k3_1, sub(kh, 2), zero], axis=0).astype(BF16)
    kg = jnp.concatenate([k3_0 * et[3], k3_1 * et[3], sub(kh, 2) * et[3], sub(kh, 3)],
                         axis=0).astype(BF16)

    a1 = _dot_nt(qh_b[SUB:2 * SUB, :], k1)
    a2 = _dot_nt(qh_b[2 * SUB:3 * SUB, :], k2)
    a3 = _dot_nt(qh_b[3 * SUB:4 * SUB, :], k3)
    a_cross = jnp.concatenate([jnp.zeros((SUB, CHUNK), F32), a1, a2, a3], axis=0)

    lane = lax.broadcasted_iota(jnp.int32, (8, CHUNK), 1)
    row8 = lax.broadcasted_iota(jnp.int32, (8, CHUNK), 0)
    blocks = []
    for j in range(N_SUB):
        base = r0 + SUB * j
        c_lo, c_hi = c2[SUB * j:SUB * j + 8, :], c2[SUB * j + 8:SUB * j + 16, :]
        q_lo, q_hi = q[SUB * j:SUB * j + 8, :], q[SUB * j + 8:SUB * j + 16, :]
        d_lo = jnp.zeros((8, CHUNK), F32)
        d_hi = jnp.zeros((8, CHUNK), F32)
        for s in range(SUB):
            lks = jnp.broadcast_to(lk_ref[hd, base + s:base + s + 1, :], (8, HEAD_DIM))
            here = lane == SUB * j + s
            col_hi = jnp.sum(q_hi * jnp.exp2(c_hi + lks), axis=-1, keepdims=True)
            d_hi = jnp.where(here, col_hi, d_hi)
            if s < 8:
                col_lo = jnp.sum(q_lo * jnp.exp2(c_lo + lks), axis=-1, keepdims=True)
                d_lo = jnp.where(here, col_lo, d_lo)
        d_lo = jnp.where(lane <= SUB * j + row8, d_lo, 0.0)
        d_hi = jnp.where(lane <= SUB * j + 8 + row8, d_hi, 0.0)
        blocks += [d_lo, d_hi]
    a = (a_cross + jnp.concatenate(blocks, axis=0)).astype(BF16)
    o_intra = _dot(a, v_b)

    st = st_ref[hd]
    o_state = _dot_nt(qg, st.astype(BF16))
    st_ref[hd] = decay[:, hs] * st + _dot_tn(v_b, kg)
    return o_state + o_intra


def _mixer_kernel(x_ref, xn_ref, g_ref, win_hbm, poolw_ref, pscale_ref, lbl_ref, hnorm_ref, wpp_hbm, whp_hbm,
                  wout_hbm, o_ref, h_ref, fz_ref, qz_ref, vz_ref, c2_ref, lk_ref, q_ref, v_ref, dec_ref, ext_ref,
                  hg_ref, pm_ref, og_ref, ga_ref, gb_ref, st_ref, win_ref, wpp_ref, whp_ref, wout_ref,
                  stage_in_ref, stage_sq_ref, sem_ref, *, layer):
    tb = pl.program_id(1)
    n_chunks = TM_MIX // CHUNK

    @pl.when((pl.program_id(0) == 0) & (tb == 0))
    def _():
        _cast_weight(win_hbm, layer, win_ref, stage_in_ref, sem_ref)
        _cast_weight(wpp_hbm, layer, wpp_ref, stage_sq_ref, sem_ref)
        _cast_weight(whp_hbm, layer, whp_ref, stage_sq_ref, sem_ref)
        _cast_weight(wout_hbm, layer, wout_ref, stage_sq_ref, sem_ref)

    lb = _lower_bound(lbl_ref, layer)
    log_lb = jnp.log(lb)
    log1m_lb = jnp.log1p(-lb)

    def tile(t):
        return slice(t * FF_TILE, (t + 1) * FF_TILE)

    def norm_job(src_ref, slot):
        def run():
            h_ref[slot] = _rmsnorm(src_ref[...], g_ref[...]).astype(BF16)
        return run

    def stage_job(dst_ref, off, t, slot):
        def run():
            dst_ref[:, tile(t)] = _dot(h_ref[slot], win_ref[:, off + t * FF_TILE:off + (t + 1) * FF_TILE])
        return run

    def stage_jobs(slot):
        return [stage_job(dst_ref, off, t, slot)
                for dst_ref, off in ((fz_ref, OFF_F), (qz_ref, OFF_Q), (vz_ref, OFF_I))
                for t in range(HGRN_W // FF_TILE)]

    def gates_job(ci):
        def run():
            rows = slice(ci * CHUNK, (ci + 1) * CHUNK)
            f = fz_ref[rows, :]
            l1p = jnp.log(1.0 + jnp.exp(-jnp.abs(f)))
            bb = log1m_lb + (jnp.minimum(f, 0.0) - l1p)
            lf2 = (jnp.maximum(log_lb, bb) + jnp.log(1.0 + jnp.exp(-jnp.abs(log_lb - bb)))) * LOG2E
            lk2 = (log1m_lb - jnp.maximum(f, 0.0) - l1p) * LOG2E
            c2 = _sub_scan(lf2)
            lk = lk2 - c2
            qc = qz_ref[rows, :]
            q = qc * _sigmoid(qc)
            vc = vz_ref[rows, :]
            for hd in range(HEADS):
                c2_ref[hd, rows, :] = c2[:, _head(hd)]
                lk_ref[hd, rows, :] = lk[:, _head(hd)]
                q_ref[hd, rows, :] = q[:, _head(hd)]
                v_ref[hd, rows, :] = vc[:, _head(hd)]
            tot = [c2[SUB * j + SUB - 1:SUB * j + SUB, :] for j in range(N_SUB)]
            p1 = tot[0]
            p2 = p1 + tot[1]
            p3 = p2 + tot[2]
            dec_ref[ci] = jnp.exp2(jnp.concatenate(
                [tot[1], tot[2], tot[3], p1, p2, p3, p3 + tot[3], p3 + tot[3]], axis=0))
        return run

    @pl.when(tb == 0)
    def _():
        st_ref[...] = jnp.zeros_like(st_ref)
        ext_ref[0:POOL_CARRY, :] = jnp.zeros((POOL_CARRY, POOL_WIDTH), F32)
        for job in [norm_job(x_ref, 0)] + stage_jobs(0) + [gates_job(ci) for ci in range(n_chunks)]:
            job()

    @pl.when(tb > 0)
    def _():
        h_ref[0] = h_ref[1]

    def proj(col0):
        return _dot(h_ref[0], win_ref[:, col0:col0 + FF_TILE])

    def pool_proj_job(t):
        def run():
            ext_ref[POOL_CARRY:POOL_CARRY + TM_MIX, tile(t)] = proj(OFF_POOL + t * FF_TILE)
        return run

    def og_job(t):
        def run():
            og = proj(OFF_OG + t * FF_TILE)
            og_ref[:, tile(t)] = og * _sigmoid(og)
        return run

    def gate_job(dst_ref, off, t, act):
        def run():
            dst_ref[:, tile(t)] = act(proj(off + t * FF_TILE))
        return run

    def pool_mix_job(gi):
        def run():
            w = POOL_WINDOWS[gi]
            cols = slice(gi * POOL_GROUP_DIM, (gi + 1) * POOL_GROUP_DIM)
            pos = lax.broadcasted_iota(jnp.int32, (POOL_CARRY, POOL_GROUP_DIM), 0) + (tb * TM_MIX + 1)
            ext = ext_ref[:, cols]
            wsum = ext
            d = 1
            while d < w:
                wsum = wsum + pltpu.roll(wsum, d, axis=0)
                d *= 2
            wsum = wsum[POOL_CARRY:, :]
            u = ext[POOL_CARRY:, :]
            inv_head = 1.0 / jnp.minimum(pos, w).astype(F32)
            mean = jnp.concatenate([wsum[:POOL_CARRY] * inv_head, wsum[POOL_CARRY:] * (1.0 / w)], axis=0)
            mixed = _dot((mean - u).astype(BF16), poolw_ref[gi].astype(BF16))
            pm_ref[:, cols] = (mixed * pscale_ref[:, cols]).astype(BF16)
        return run

    def pool_merge_job():
        ga_ref[...] = ga_ref[...] * _dot(pm_ref[...], wpp_ref[...])
        ext_ref[0:POOL_CARRY, :] = ext_ref[TM_MIX:TM_MIX + POOL_CARRY, :]

    og_jobs = [og_job(t) for t in range(HGRN_W // FF_TILE)]
    pool_proj_jobs = [pool_proj_job(t) for t in range(POOL_WIDTH // FF_TILE)]
    ga_jobs = [gate_job(ga_ref, OFF_GA, t, _sigmoid) for t in range(D_MODEL // FF_TILE)]
    gb_jobs = [gate_job(gb_ref, OFF_GB, t, lambda z: z) for t in range(D_MODEL // FF_TILE)]
    mix_jobs = [pool_mix_job(gi) for gi in range(len(POOL_WINDOWS))]
    nxt = stage_jobs(1)
    schedule = [
        og_jobs + [norm_job(xn_ref, 1)],
        nxt[0:3],
        nxt[3:6],
        pool_proj_jobs + [gates_job(0)],
        ga_jobs[0:2] + [gates_job(1), gates_job(2)],
        ga_jobs[2:4] + mix_jobs[0:2] + [gates_job(3)],
        gb_jobs[0:2] + mix_jobs[2:4] + [pool_merge_job, gates_job(4)],
        gb_jobs[2:4] + [gates_job(5)],
    ]
    assert len(schedule) == n_chunks
    gates_after_loop = (n_chunks - 2, n_chunks - 1)

    for ci in range(n_chunks):
        for job in schedule[ci]:
            job()
        r0 = ci * CHUNK
        dec = dec_ref[ci]
        e_tot = [None] + [dec[j - 1:j, :] for j in range(1, N_SUB)]
        e_pre = [None] + [dec[N_SUB - 2 + j:N_SUB - 1 + j, :] for j in range(1, N_SUB)]
        decay = dec[2 * N_SUB - 2:2 * N_SUB - 1, :]
        outs = [_hgrn_chunk_head(r0, hd, c2_ref, lk_ref, q_ref, v_ref, st_ref, e_tot, e_pre, decay)
                for hd in range(HEADS)]
        normed = []
        for oh in outs:
            ms = jnp.mean(oh * oh, axis=-1, keepdims=True)
            normed.append(oh * lax.rsqrt(ms + EPS))
        on = jnp.concatenate(normed, axis=-1) * hnorm_ref[...]
        hg_ref[r0:r0 + CHUNK, :] = (on * og_ref[r0:r0 + CHUNK, :]).astype(BF16)

    for k in gates_after_loop:
        gates_job(k)()
    pb = _dot(hg_ref[...], whp_ref[...])
    merged = (ga_ref[...] + _sigmoid(gb_ref[...]) * pb).astype(BF16)
    o_ref[...] = x_ref[...] + _dot(merged, wout_ref[...])


def _mixer(x, g, win, poolw, pscale, lbl, hnorm, wpp, whp, wout, layer, batch):
    n = x.shape[0]
    nt = n // batch // TM_MIX
    depth = lbl.shape[0]
    hbm = pl.BlockSpec(memory_space=pl.ANY)
    return pl.pallas_call(
        functools.partial(_mixer_kernel, layer=layer),
        grid=(batch, nt),
        in_specs=[
            pl.BlockSpec((TM_MIX, D_MODEL), lambda b, t: (b * nt + t, 0)),
            pl.BlockSpec((TM_MIX, D_MODEL), lambda b, t: (b * nt + jnp.minimum(t + 1, nt - 1), 0)),
            _const_spec((1, D_MODEL)),
            hbm,
            _const_spec((len(POOL_WINDOWS), POOL_GROUP_DIM, POOL_GROUP_DIM)),
            _const_spec((1, POOL_WIDTH)),
            _const_spec((depth, HGRN_W)),
            _const_spec((1, HGRN_W)),
            hbm, hbm, hbm,
        ],
        out_specs=pl.BlockSpec((TM_MIX, D_MODEL), lambda b, t: (b * nt + t, 0)),
        out_shape=jax.ShapeDtypeStruct((n, D_MODEL), F32),
        scratch_shapes=[
            pltpu.VMEM((2, TM_MIX, D_MODEL), BF16),
            pltpu.VMEM((TM_MIX, HGRN_W), F32),
            pltpu.VMEM((TM_MIX, HGRN_W), F32),
            pltpu.VMEM((TM_MIX, HGRN_W), F32),
            pltpu.VMEM((HEADS, TM_MIX, HEAD_DIM), F32),
            pltpu.VMEM((HEADS, TM_MIX, HEAD_DIM), F32),
            pltpu.VMEM((HEADS, TM_MIX, HEAD_DIM), F32),
            pltpu.VMEM((HEADS, TM_MIX, HEAD_DIM), F32),
            pltpu.VMEM((TM_MIX // CHUNK, 8, HGRN_W), F32),
            pltpu.VMEM((POOL_CARRY + TM_MIX, POOL_WIDTH), F32),
            pltpu.VMEM((TM_MIX, HGRN_W), BF16),
            pltpu.VMEM((TM_MIX, POOL_WIDTH), BF16),
            pltpu.VMEM((TM_MIX, HGRN_W), F32),
            pltpu.VMEM((TM_MIX, D_MODEL), F32),
            pltpu.VMEM((TM_MIX, D_MODEL), F32),
            pltpu.VMEM((HEADS, HEAD_DIM, HEAD_DIM), F32),
            pltpu.VMEM((D_MODEL, D_IN), BF16),
            pltpu.VMEM((POOL_WIDTH, D_MODEL), BF16),
            pltpu.VMEM((HGRN_W, D_MODEL), BF16),
            pltpu.VMEM((D_MODEL, D_MODEL), BF16),
            pltpu.VMEM((2, CAST_ROWS_IN_PROJ, D_IN), F32),
            pltpu.VMEM((2, CAST_ROWS_SQUARE, D_MODEL), F32),
            pltpu.SemaphoreType.DMA((2,)),
        ],
        compiler_params=pltpu.CompilerParams(
            dimension_semantics=("arbitrary", "arbitrary"), vmem_limit_bytes=VMEM_LIMIT_BYTES),
        name="hybrid_mixer",
    )(x, x, g, win, poolw, pscale, lbl, hnorm, wpp, whp, wout)


def kernel(x, ffn1_norm, ffn1_w_gate, ffn1_w_up, ffn1_w_down, mix_norm, w_in, pool_w, pool_scale,
           lb_logits, hgrn_norm, w_pool_proj, w_hgrn_proj, w_out, ffn2_norm, ffn2_w_gate, ffn2_w_up,
           ffn2_w_down, final_norm):
    batch, seq, _ = x.shape
    depth = ffn1_norm.shape[0]
    xf = x.reshape(batch * seq, D_MODEL)
    fg = final_norm.reshape(1, D_MODEL)
    for l in range(depth):
        xf = _ffn(xf, ffn1_norm[l].reshape(1, D_MODEL), ffn1_w_gate, ffn1_w_up, ffn1_w_down, fg, l, False)
        xf = _mixer(xf, mix_norm[l].reshape(1, D_MODEL), w_in, pool_w[l],
                    pool_scale[l].reshape(1, POOL_WIDTH), lb_logits, hgrn_norm[l].reshape(1, HGRN_W),
                    w_pool_proj, w_hgrn_proj, w_out, l, batch)
        xf = _ffn(xf, ffn2_norm[l].reshape(1, D_MODEL), ffn2_w_gate, ffn2_w_up, ffn2_w_down, fg, l,
                  l == depth - 1)
    return xf.reshape(batch, seq, D_MODEL)
```

```python
import functools

import jax
import jax.numpy as jnp
from jax import lax
from jax.experimental import pallas as pl
from jax.experimental.pallas import tpu as pltpu

D_MODEL = 1024
D_FF = 2816
CHUNK = 64
SUB = 16
N_SUB = CHUNK // SUB
POOL_WIDTH = 512
POOL_WINDOWS = (2, 4, 8, 16)
POOL_GROUP_DIM = 128
POOL_CARRY = 16
HEADS = 4
HEAD_DIM = 128
HGRN_W = HEADS * HEAD_DIM
EPS = 1e-6
LOG2E = 1.4426950408889634

FF_TILE = 256
N_FF_TILES = D_FF // FF_TILE
TM_FFN = 512
TM_MIX = 512
CAST_ROWS_IN_PROJ = 64
CAST_ROWS_SQUARE = 256
VMEM_LIMIT_BYTES = 56 * 1024 * 1024

OFF_POOL = 0
OFF_Q = POOL_WIDTH
OFF_F = OFF_Q + HGRN_W
OFF_I = OFF_F + HGRN_W
OFF_OG = OFF_I + HGRN_W
OFF_GA = OFF_OG + HGRN_W
OFF_GB = OFF_GA + D_MODEL
D_IN = OFF_GB + D_MODEL

F32 = jnp.float32
BF16 = jnp.bfloat16


def _dot(a, b):
    return jnp.dot(a, b, preferred_element_type=F32)


def _dot_nt(a, b):
    return lax.dot_general(a, b, (((1,), (1,)), ((), ())), preferred_element_type=F32)


def _dot_tn(a, b):
    return lax.dot_general(a, b, (((0,), (0,)), ((), ())), preferred_element_type=F32)


def _rmsnorm(x, g):
    ms = jnp.mean(x * x, axis=-1, keepdims=True)
    return x * lax.rsqrt(ms + EPS) * g


def _sigmoid(x):
    return 1.0 / (1.0 + jnp.exp(-x))


def _weight_chunk_copy(src_hbm, layer, stage_ref, sem_ref, c):
    rows = stage_ref.shape[1]
    return pltpu.make_async_copy(src_hbm.at[layer, pl.ds(c * rows, rows), :], stage_ref.at[c % 2],
                                 sem_ref.at[c % 2])


def _cast_weight(src_hbm, layer, dst_ref, stage_ref, sem_ref):
    rows = stage_ref.shape[1]
    n_chunks, rem = divmod(dst_ref.shape[0], rows)
    assert rem == 0
    _weight_chunk_copy(src_hbm, layer, stage_ref, sem_ref, 0).start()
    for c in range(n_chunks):
        if c + 1 < n_chunks:
            _weight_chunk_copy(src_hbm, layer, stage_ref, sem_ref, c + 1).start()
        _weight_chunk_copy(src_hbm, layer, stage_ref, sem_ref, c).wait()
        dst_ref[c * rows:(c + 1) * rows, :] = stage_ref[c % 2].astype(BF16)


def _ffn_kernel(x_ref, g_ref, wg_hbm, wu_hbm, wd_hbm, fg_ref, o_ref, h_ref, a_ref, wg_ref, wu_ref, wd_ref,
                stage_in_ref, stage_out_ref, sem_ref, *, layer, final_norm):
    def tile_copies(j):
        k = j % 2
        cols = pl.ds(j * FF_TILE, FF_TILE)
        return (pltpu.make_async_copy(wg_hbm.at[layer, :, cols], stage_in_ref.at[0, k], sem_ref.at[0, k]),
                pltpu.make_async_copy(wu_hbm.at[layer, :, cols], stage_in_ref.at[1, k], sem_ref.at[1, k]),
                pltpu.make_async_copy(wd_hbm.at[layer, cols, :], stage_out_ref.at[k], sem_ref.at[2, k]))

    def fetch_tile(j):
        for copy in tile_copies(j):
            copy.start()

    def convert_tile(j):
        cols = slice(j * FF_TILE, (j + 1) * FF_TILE)
        if j + 1 < N_FF_TILES:
            fetch_tile(j + 1)
        for copy in tile_copies(j):
            copy.wait()
        wg_ref[:, cols] = stage_in_ref[0, j % 2].astype(BF16)
        wu_ref[:, cols] = stage_in_ref[1, j % 2].astype(BF16)
        wd_ref[cols, :] = stage_out_ref[j % 2].astype(BF16)

    def half_step(before_tile):
        x = x_ref[...]
        h_ref[...] = _rmsnorm(x, g_ref[...]).astype(BF16)
        for j in range(N_FF_TILES):
            before_tile(j)
            cols = slice(j * FF_TILE, (j + 1) * FF_TILE)
            h = h_ref[...]
            g = _dot(h, wg_ref[:, cols])
            u = _dot(h, wu_ref[:, cols])
            a_ref[:, cols] = (g * _sigmoid(g) * u).astype(BF16)
        y = x + 0.5 * _dot(a_ref[...], wd_ref[...])
        if final_norm:
            y = _rmsnorm(y, fg_ref[...])
        o_ref[...] = y

    @pl.when(pl.program_id(0) == 0)
    def _():
        fetch_tile(0)
        half_step(convert_tile)

    @pl.when(pl.program_id(0) > 0)
    def _():
        half_step(lambda j: None)


def _const_spec(shape):
    nd = len(shape)
    return pl.BlockSpec(shape, lambda *_: (0,) * nd, pipeline_mode=pl.Buffered(1))


def _ffn(x, g, wg, wu, wd, fg, layer, final_norm):
    n = x.shape[0]
    hbm = pl.BlockSpec(memory_space=pl.ANY)
    return pl.pallas_call(
        functools.partial(_ffn_kernel, layer=layer, final_norm=final_norm),
        grid=(n // TM_FFN,),
        in_specs=[
            pl.BlockSpec((TM_FFN, D_MODEL), lambda i: (i, 0)),
            _const_spec((1, D_MODEL)),
            hbm, hbm, hbm,
            _const_spec((1, D_MODEL)),
        ],
        out_specs=pl.BlockSpec((TM_FFN, D_MODEL), lambda i: (i, 0)),
        out_shape=jax.ShapeDtypeStruct((n, D_MODEL), F32),
        scratch_shapes=[
            pltpu.VMEM((TM_FFN, D_MODEL), BF16),
            pltpu.VMEM((TM_FFN, D_FF), BF16),
            pltpu.VMEM((D_MODEL, D_FF), BF16),
            pltpu.VMEM((D_MODEL, D_FF), BF16),
            pltpu.VMEM((D_FF, D_MODEL), BF16),
            pltpu.VMEM((2, 2, D_MODEL, FF_TILE), F32),
            pltpu.VMEM((2, FF_TILE, D_MODEL), F32),
            pltpu.SemaphoreType.DMA((3, 2)),
        ],
        compiler_params=pltpu.CompilerParams(
            dimension_semantics=("arbitrary",), vmem_limit_bytes=VMEM_LIMIT_BYTES),
        name="swiglu_half_step",
    )(x, g, wg, wu, wd, fg)


def _lower_bound(lbl_ref, layer):
    logits = lbl_ref[...]
    m = jnp.max(logits, axis=0, keepdims=True)
    e = jnp.exp(logits - m)
    p = e / jnp.sum(e, axis=0, keepdims=True)
    lb = jnp.zeros((1, HGRN_W), F32)
    for j in range(1, layer + 1):
        lb = lb + p[j:j + 1, :]
    return lb


def _head(hd):
    return slice(hd * HEAD_DIM, (hd + 1) * HEAD_DIM)


def _sub_scan(lf2):
    n8 = CHUNK // 8
    w = lf2.reshape(n8, 8, HGRN_W)
    sub8 = lax.broadcasted_iota(jnp.int32, (n8, 8, HGRN_W), 1)
    for d in (1, 2, 4):
        w = w + jnp.where(sub8 >= d, pltpu.roll(w, d, axis=1), 0.0)
    w = w.reshape(CHUNK, HGRN_W)
    parts = []
    for j in range(N_SUB):
        lo = w[SUB * j:SUB * j + 8, :]
        hi = w[SUB * j + 8:SUB * j + 16, :] + lo[7:8, :]
        parts += [lo, hi]
    return jnp.concatenate(parts, axis=0)


def _hgrn_chunk_head(r0, hd, c2_ref, lk_ref, q_ref, v_ref, st_ref, e_tot, e_pre, decay):
    hs = _head(hd)
    rows = slice(r0, r0 + CHUNK)
    c2 = c2_ref[hd, rows, :]
    lk = lk_ref[hd, rows, :]
    q = q_ref[hd, rows, :]
    v_b = v_ref[hd, rows, :].astype(BF16)

    def sub(a, j):
        return a[SUB * j:SUB * (j + 1), :]

    tot = [c2[SUB * j + SUB - 1:SUB * j + SUB, :] for j in range(N_SUB)]
    tot_own = jnp.concatenate([jnp.broadcast_to(t, (SUB, HEAD_DIM)) for t in tot], axis=0)
    qh = q * jnp.exp2(c2)
    kh = jnp.exp2(lk + tot_own)
    et = [None] + [e_tot[j][:, hs] for j in range(1, N_SUB)]
    qg = jnp.concatenate([sub(qh, 0)] + [sub(qh, j) * e_pre[j][:, hs] for j in range(1, N_SUB)],
                         axis=0).astype(BF16)
    qh_b = qh.astype(BF16)

    zero = jnp.zeros((SUB, HEAD_DIM), F32)
    k2_0 = sub(kh, 0) * et[1]
    k3_0 = k2_0 * et[2]
    k3_1 = sub(kh, 1) * et[2]
    k1 = jnp.concatenate([sub(kh, 0), zero, zero, zero], axis=0).astype(BF16)
    k2 = jnp.concatenate([k2_0, sub(kh, 1), zero, zero], axis=0).astype(BF16)
    k3 = jnp.concatenate([k3_0, k3_1, sub(kh, 2), zero], axis=0).astype(BF16)
    kg = jnp.concatenate([k3_0 * et[3], k3_1 * et[3], sub(kh, 2) * et[3], sub(kh, 3)],
                         axis=0).astype(BF16)

    a1 = _dot_nt(qh_b[SUB:2 * SUB, :], k1)
    a2 = _dot_nt(qh_b[2 * SUB:3 * SUB, :], k2)
    a3 = _dot_nt(qh_b[3 * SUB:4 * SUB, :], k3)
    a_cross = jnp.concatenate([jnp.zeros((SUB, CHUNK), F32), a1, a2, a3], axis=0)

    lane = lax.broadcasted_iota(jnp.int32, (8, CHUNK), 1)
    row8 = lax.broadcasted_iota(jnp.int32, (8, CHUNK), 0)
    blocks = []
    for j in range(N_SUB):
        base = r0 + SUB * j
        c_lo, c_hi = c2[SUB * j:SUB * j + 8, :], c2[SUB * j + 8:SUB * j + 16, :]
        q_lo, q_hi = q[SUB * j:SUB * j + 8, :], q[SUB * j + 8:SUB * j + 16, :]
        d_lo = jnp.zeros((8, CHUNK), F32)
        d_hi = jnp.zeros((8, CHUNK), F32)
        for s in range(SUB):
            lks = jnp.broadcast_to(lk_ref[hd, base + s:base + s + 1, :], (8, HEAD_DIM))
            here = lane == SUB * j + s
            col_hi = jnp.sum(q_hi * jnp.exp2(c_hi + lks), axis=-1, keepdims=True)
            d_hi = jnp.where(here, col_hi, d_hi)
            if s < 8:
                col_lo = jnp.sum(q_lo * jnp.exp2(c_lo + lks), axis=-1, keepdims=True)
                d_lo = jnp.where(here, col_lo, d_lo)
        d_lo = jnp.where(lane <= SUB * j + row8, d_lo, 0.0)
        d_hi = jnp.where(lane <= SUB * j + 8 + row8, d_hi, 0.0)
        blocks += [d_lo, d_hi]
    a = (a_cross + jnp.concatenate(blocks, axis=0)).astype(BF16)
    o_intra = _dot(a, v_b)

    st = st_ref[hd]
    o_state = _dot_nt(qg, st.astype(BF16))
    st_ref[hd] = decay[:, hs] * st + _dot_tn(v_b, kg)
    return o_state + o_intra


def _mixer_kernel(x_ref, xn_ref, g_ref, win_hbm, poolw_ref, pscale_ref, lbl_ref, hnorm_ref, wpp_hbm, whp_hbm,
                  wout_hbm, o_ref, h_ref, fz_ref, qz_ref, vz_ref, c2_ref, lk_ref, q_ref, v_ref, dec_ref, ext_ref,
                  hg_ref, pm_ref, og_ref, ga_ref, gb_ref, st_ref, win_ref, wpp_ref, whp_ref, wout_ref,
                  stage_in_ref, stage_sq_ref, sem_ref, *, layer):
    tb = pl.program_id(1)
    n_chunks = TM_MIX // CHUNK

    @pl.when((pl.program_id(0) == 0) & (tb == 0))
    def _():
        _cast_weight(win_hbm, layer, win_ref, stage_in_ref, sem_ref)
        _cast_weight(wpp_hbm, layer, wpp_ref, stage_sq_ref, sem_ref)
        _cast_weight(whp_hbm, layer, whp_ref, stage_sq_ref, sem_ref)
        _cast_weight(wout_hbm, layer, wout_ref, stage_sq_ref, sem_ref)

    lb = _lower_bound(lbl_ref, layer)
    log_lb = jnp.log(lb)
    log1m_lb = jnp.log1p(-lb)

    def tile(t):
        return slice(t * FF_TILE, (t + 1) * FF_TILE)

    def norm_job(src_ref, slot):
        def run():
            h_ref[slot] = _rmsnorm(src_ref[...], g_ref[...]).astype(BF16)
        return run

    def stage_job(dst_ref, off, t, slot):
        def run():
            dst_ref[:, tile(t)] = _dot(h_ref[slot], win_ref[:, off + t * FF_TILE:off + (t + 1) * FF_TILE])
        return run

    def stage_jobs(slot):
        return [stage_job(dst_ref, off, t, slot)
                for dst_ref, off in ((fz_ref, OFF_F), (qz_ref, OFF_Q), (vz_ref, OFF_I))
                for t in range(HGRN_W // FF_TILE)]

    def gates_job(ci):
        def run():
            rows = slice(ci * CHUNK, (ci + 1) * CHUNK)
            f = fz_ref[rows, :]
            l1p = jnp.log(1.0 + jnp.exp(-jnp.abs(f)))
            bb = log1m_lb + (jnp.minimum(f, 0.0) - l1p)
            lf2 = (jnp.maximum(log_lb, bb) + jnp.log(1.0 + jnp.exp(-jnp.abs(log_lb - bb)))) * LOG2E
            lk2 = (log1m_lb - jnp.maximum(f, 0.0) - l1p) * LOG2E
            c2 = _sub_scan(lf2)
            lk = lk2 - c2
            qc = qz_ref[rows, :]
            q = qc * _sigmoid(qc)
            vc = vz_ref[rows, :]
            for hd in range(HEADS):
                c2_ref[hd, rows, :] = c2[:, _head(hd)]
                lk_ref[hd, rows, :] = lk[:, _head(hd)]
                q_ref[hd, rows, :] = q[:, _head(hd)]
                v_ref[hd, rows, :] = vc[:, _head(hd)]
            tot = [c2[SUB * j + SUB - 1:SUB * j + SUB, :] for j in range(N_SUB)]
            p1 = tot[0]
            p2 = p1 + tot[1]
            p3 = p2 + tot[2]
            dec_ref[ci] = jnp.exp2(jnp.concatenate(
                [tot[1], tot[2], tot[3], p1, p2, p3, p3 + tot[3], p3 + tot[3]], axis=0))
        return run

    @pl.when(tb == 0)
    def _():
        st_ref[...] = jnp.zeros_like(st_ref)
        ext_ref[0:POOL_CARRY, :] = jnp.zeros((POOL_CARRY, POOL_WIDTH), F32)
        for job in [norm_job(x_ref, 0)] + stage_jobs(0) + [gates_job(ci) for ci in range(n_chunks)]:
            job()

    @pl.when(tb > 0)
    def _():
        h_ref[0] = h_ref[1]

    def proj(col0):
        return _dot(h_ref[0], win_ref[:, col0:col0 + FF_TILE])

    def pool_proj_job(t):
        def run():
            ext_ref[POOL_CARRY:POOL_CARRY + TM_MIX, tile(t)] = proj(OFF_POOL + t * FF_TILE)
        return run

    def og_job(t):
        def run():
            og = proj(OFF_OG + t * FF_TILE)
            og_ref[:, tile(t)] = og * _sigmoid(og)
        return run

    def gate_job(dst_ref, off, t, act):
        def run():
            dst_ref[:, tile(t)] = act(proj(off + t * FF_TILE))
        return run

    def pool_mix_job(gi):
        def run():
            w = POOL_WINDOWS[gi]
            cols = slice(gi * POOL_GROUP_DIM, (gi + 1) * POOL_GROUP_DIM)
            pos = lax.broadcasted_iota(jnp.int32, (POOL_CARRY, POOL_GROUP_DIM), 0) + (tb * TM_MIX + 1)
            ext = ext_ref[:, cols]
            wsum = ext
            d = 1
            while d < w:
                wsum = wsum + pltpu.roll(wsum, d, axis=0)
                d *= 2
            wsum = wsum[POOL_CARRY:, :]
            u = ext[POOL_CARRY:, :]
            inv_head = 1.0 / jnp.minimum(pos, w).astype(F32)
            mean = jnp.concatenate([wsum[:POOL_CARRY] * inv_head, wsum[POOL_CARRY:] * (1.0 / w)], axis=0)
            mixed = _dot((mean - u).astype(BF16), poolw_ref[gi].astype(BF16))
            pm_ref[:, cols] = (mixed * pscale_ref[:, cols]).astype(BF16)
        return run

    def pool_merge_job():
        ga_ref[...] = ga_ref[...] * _dot(pm_ref[...], wpp_ref[...])
        ext_ref[0:POOL_CARRY, :] = ext_ref[TM_MIX:TM_MIX + POOL_CARRY, :]

    og_jobs = [og_job(t) for t in range(HGRN_W // FF_TILE)]
    pool_proj_jobs = [pool_proj_job(t) for t in range(POOL_WIDTH // FF_TILE)]
    ga_jobs = [gate_job(ga_ref, OFF_GA, t, _sigmoid) for t in range(D_MODEL // FF_TILE)]
    gb_jobs = [gate_job(gb_ref, OFF_GB, t, lambda z: z) for t in range(D_MODEL // FF_TILE)]
    mix_jobs = [pool_mix_job(gi) for gi in range(len(POOL_WINDOWS))]
    nxt = stage_jobs(1)
    schedule = [
        og_jobs + [norm_job(xn_ref, 1)] + nxt[0:1],
        nxt[1:4],
        nxt[4:6] + pool_proj_jobs[0:1],
        pool_proj_jobs[1:2] + ga_jobs[0:2] + [gates_job(0)],
        ga_jobs[2:4] + gb_jobs[0:1] + [gates_job(1), gates_job(2)],
        gb_jobs[1:3] + mix_jobs[0:2] + [gates_job(3)],
        gb_jobs[3:4] + mix_jobs[2:4] + [pool_merge_job, gates_job(4)],
        [gates_job(5)],
    ]
    assert len(schedule) == n_chunks
    gates_after_loop = (n_chunks - 2, n_chunks - 1)

    for ci in range(n_chunks):
        for job in schedule[ci]:
            job()
        r0 = ci * CHUNK
        dec = dec_ref[ci]
        e_tot = [None] + [dec[j - 1:j, :] for j in range(1, N_SUB)]
        e_pre = [None] + [dec[N_SUB - 2 + j:N_SUB - 1 + j, :] for j in range(1, N_SUB)]
        decay = dec[2 * N_SUB - 2:2 * N_SUB - 1, :]
        outs = [_hgrn_chunk_head(r0, hd, c2_ref, lk_ref, q_ref, v_ref, st_ref, e_tot, e_pre, decay)
                for hd in range(HEADS)]
        normed = []
        for oh in outs:
            ms = jnp.mean(oh * oh, axis=-1, keepdims=True)
            normed.append(oh * lax.rsqrt(ms + EPS))
        on = jnp.concatenate(normed, axis=-1) * hnorm_ref[...]
        hg_ref[r0:r0 + CHUNK, :] = (on * og_ref[r0:r0 + CHUNK, :]).astype(BF16)

    for k in gates_after_loop:
        gates_job(k)()
    pb = _dot(hg_ref[...], whp_ref[...])
    merged = (ga_ref[...] + _sigmoid(gb_ref[...]) * pb).astype(BF16)
    o_ref[...] = x_ref[...] + _dot(merged, wout_ref[...])


def _mixer(x, g, win, poolw, pscale, lbl, hnorm, wpp, whp, wout, layer, batch):
    n = x.shape[0]
    nt = n // batch // TM_MIX
    depth = lbl.shape[0]
    hbm = pl.BlockSpec(memory_space=pl.ANY)
    return pl.pallas_call(
        functools.partial(_mixer_kernel, layer=layer),
        grid=(batch, nt),
        in_specs=[
            pl.BlockSpec((TM_MIX, D_MODEL), lambda b, t: (b * nt + t, 0)),
            pl.BlockSpec((TM_MIX, D_MODEL), lambda b, t: (b * nt + jnp.minimum(t + 1, nt - 1), 0)),
            _const_spec((1, D_MODEL)),
            hbm,
            _const_spec((len(POOL_WINDOWS), POOL_GROUP_DIM, POOL_GROUP_DIM)),
            _const_spec((1, POOL_WIDTH)),
            _const_spec((depth, HGRN_W)),
            _const_spec((1, HGRN_W)),
            hbm, hbm, hbm,
        ],
        out_specs=pl.BlockSpec((TM_MIX, D_MODEL), lambda b, t: (b * nt + t, 0)),
        out_shape=jax.ShapeDtypeStruct((n, D_MODEL), F32),
        scratch_shapes=[
            pltpu.VMEM((2, TM_MIX, D_MODEL), BF16),
            pltpu.VMEM((TM_MIX, HGRN_W), F32),
            pltpu.VMEM((TM_MIX, HGRN_W), F32),
            pltpu.VMEM((TM_MIX, HGRN_W), F32),
            pltpu.VMEM((HEADS, TM_MIX, HEAD_DIM), F32),
            pltpu.VMEM((HEADS, TM_MIX, HEAD_DIM), F32),
            pltpu.VMEM((HEADS, TM_MIX, HEAD_DIM), F32),
            pltpu.VMEM((HEADS, TM_MIX, HEAD_DIM), F32),
            pltpu.VMEM((TM_MIX // CHUNK, 8, HGRN_W), F32),
            pltpu.VMEM((POOL_CARRY + TM_MIX, POOL_WIDTH), F32),
            pltpu.VMEM((TM_MIX, HGRN_W), BF16),
            pltpu.VMEM((TM_MIX, POOL_WIDTH), BF16),
            pltpu.VMEM((TM_MIX, HGRN_W), F32),
            pltpu.VMEM((TM_MIX, D_MODEL), F32),
            pltpu.VMEM((TM_MIX, D_MODEL), F32),
            pltpu.VMEM((HEADS, HEAD_DIM, HEAD_DIM), F32),
            pltpu.VMEM((D_MODEL, D_IN), BF16),
            pltpu.VMEM((POOL_WIDTH, D_MODEL), BF16),
            pltpu.VMEM((HGRN_W, D_MODEL), BF16),
            pltpu.VMEM((D_MODEL, D_MODEL), BF16),
            pltpu.VMEM((2, CAST_ROWS_IN_PROJ, D_IN), F32),
            pltpu.VMEM((2, CAST_ROWS_SQUARE, D_MODEL), F32),
            pltpu.SemaphoreType.DMA((2,)),
        ],
        compiler_params=pltpu.CompilerParams(
            dimension_semantics=("arbitrary", "arbitrary"), vmem_limit_bytes=VMEM_LIMIT_BYTES),
        name="hybrid_mixer",
    )(x, x, g, win, poolw, pscale, lbl, hnorm, wpp, whp, wout)


def kernel(x, ffn1_norm, ffn1_w_gate, ffn1_w_up, ffn1_w_down, mix_norm, w_in, pool_w, pool_scale,
           lb_logits, hgrn_norm, w_pool_proj, w_hgrn_proj, w_out, ffn2_norm, ffn2_w_gate, ffn2_w_up,
           ffn2_w_down, final_norm):
    batch, seq, _ = x.shape
    depth = ffn1_norm.shape[0]
    xf = x.reshape(batch * seq, D_MODEL)
    fg = final_norm.reshape(1, D_MODEL)
    for l in range(depth):
        xf = _ffn(xf, ffn1_norm[l].reshape(1, D_MODEL), ffn1_w_gate, ffn1_w_up, ffn1_w_down, fg, l, False)
        xf = _mixer(xf, mix_norm[l].reshape(1, D_MODEL), w_in, pool_w[l],
                    pool_scale[l].reshape(1, POOL_WIDTH), lb_logits, hgrn_norm[l].reshape(1, HGRN_W),
                    w_pool_proj, w_hgrn_proj, w_out, l, batch)
        xf = _ffn(xf, ffn2_norm[l].reshape(1, D_MODEL), ffn2_w_gate, ffn2_w_up, ffn2_w_down, fg, l,
                  l == depth - 1)
    return xf.reshape(batch, seq, D_MODEL)
```

```python
import functools

import jax
import jax.numpy as jnp
from jax import lax
from jax.experimental import pallas as pl
from jax.experimental.pallas import tpu as pltpu

D_MODEL = 1024
D_FF = 2816
CHUNK = 64
SUB = 16
N_SUB = CHUNK // SUB
POOL_WIDTH = 512
POOL_WINDOWS = (2, 4, 8, 16)
POOL_GROUP_DIM = 128
POOL_CARRY = 16
HEADS = 4
HEAD_DIM = 128
HGRN_W = HEADS * HEAD_DIM
EPS = 1e-6
LOG2E = 1.4426950408889634

FF_TILE = 256
N_FF_TILES = D_FF // FF_TILE
TM_FFN = 512
TM_MIX = 512
CAST_ROWS_IN_PROJ = 64
CAST_ROWS_SQUARE = 256
VMEM_LIMIT_BYTES = 56 * 1024 * 1024

OFF_POOL = 0
OFF_Q = POOL_WIDTH
OFF_F = OFF_Q + HGRN_W
OFF_I = OFF_F + HGRN_W
OFF_OG = OFF_I + HGRN_W
OFF_GA = OFF_OG + HGRN_W
OFF_GB = OFF_GA + D_MODEL
D_IN = OFF_GB + D_MODEL

F32 = jnp.float32
BF16 = jnp.bfloat16


def _dot(a, b):
    return jnp.dot(a, b, preferred_element_type=F32)


def _dot_nt(a, b):
    return lax.dot_general(a, b, (((1,), (1,)), ((), ())), preferred_element_type=F32)


def _dot_tn(a, b):
    return lax.dot_general(a, b, (((0,), (0,)), ((), ())), preferred_element_type=F32)


def _rmsnorm(x, g):
    ms = jnp.mean(x * x, axis=-1, keepdims=True)
    return x * lax.rsqrt(ms + EPS) * g


def _sigmoid(x):
    return 1.0 / (1.0 + jnp.exp(-x))


def _weight_chunk_copy(src_hbm, layer, stage_ref, sem_ref, c):
    rows = stage_ref.shape[1]
    return pltpu.make_async_copy(src_hbm.at[layer, pl.ds(c * rows, rows), :], stage_ref.at[c % 2],
                                 sem_ref.at[c % 2])


def _cast_weight(src_hbm, layer, dst_ref, stage_ref, sem_ref):
    rows = stage_ref.shape[1]
    n_chunks, rem = divmod(dst_ref.shape[0], rows)
    assert rem == 0
    _weight_chunk_copy(src_hbm, layer, stage_ref, sem_ref, 0).start()
    for c in range(n_chunks):
        if c + 1 < n_chunks:
            _weight_chunk_copy(src_hbm, layer, stage_ref, sem_ref, c + 1).start()
        _weight_chunk_copy(src_hbm, layer, stage_ref, sem_ref, c).wait()
        dst_ref[c * rows:(c + 1) * rows, :] = stage_ref[c % 2].astype(BF16)


def _ffn_kernel(x_ref, g_ref, wg_hbm, wu_hbm, wd_hbm, fg_ref, o_ref, h_ref, a_ref, wg_ref, wu_ref, wd_ref,
                stage_in_ref, stage_out_ref, sem_ref, *, layer, final_norm):
    def tile_copies(j):
        k = j % 2
        cols = pl.ds(j * FF_TILE, FF_TILE)
        return (pltpu.make_async_copy(wg_hbm.at[layer, :, cols], stage_in_ref.at[0, k], sem_ref.at[0, k]),
                pltpu.make_async_copy(wu_hbm.at[layer, :, cols], stage_in_ref.at[1, k], sem_ref.at[1, k]),
                pltpu.make_async_copy(wd_hbm.at[layer, cols, :], stage_out_ref.at[k], sem_ref.at[2, k]))

    def fetch_tile(j):
        for copy in tile_copies(j):
            copy.start()

    def convert_tile(j):
        cols = slice(j * FF_TILE, (j + 1) * FF_TILE)
        if j + 1 < N_FF_TILES:
            fetch_tile(j + 1)
        for copy in tile_copies(j):
            copy.wait()
        wg_ref[:, cols] = stage_in_ref[0, j % 2].astype(BF16)
        wu_ref[:, cols] = stage_in_ref[1, j % 2].astype(BF16)
        wd_ref[cols, :] = stage_out_ref[j % 2].astype(BF16)

    def half_step(before_tile):
        x = x_ref[...]
        h_ref[...] = _rmsnorm(x, g_ref[...]).astype(BF16)
        for j in range(N_FF_TILES):
            before_tile(j)
            cols = slice(j * FF_TILE, (j + 1) * FF_TILE)
            h = h_ref[...]
            g = _dot(h, wg_ref[:, cols])
            u = _dot(h, wu_ref[:, cols])
            a_ref[:, cols] = (g * _sigmoid(g) * u).astype(BF16)
        y = x + 0.5 * _dot(a_ref[...], wd_ref[...])
        if final_norm:
            y = _rmsnorm(y, fg_ref[...])
        o_ref[...] = y

    @pl.when(pl.program_id(0) == 0)
    def _():
        fetch_tile(0)
        half_step(convert_tile)

    @pl.when(pl.program_id(0) > 0)
    def _():
        half_step(lambda j: None)


def _const_spec(shape):
    nd = len(shape)
    return pl.BlockSpec(shape, lambda *_: (0,) * nd, pipeline_mode=pl.Buffered(1))


def _ffn(x, g, wg, wu, wd, fg, layer, final_norm):
    n = x.shape[0]
    hbm = pl.BlockSpec(memory_space=pl.ANY)
    return pl.pallas_call(
        functools.partial(_ffn_kernel, layer=layer, final_norm=final_norm),
        grid=(n // TM_FFN,),
        in_specs=[
            pl.BlockSpec((TM_FFN, D_MODEL), lambda i: (i, 0)),
            _const_spec((1, D_MODEL)),
            hbm, hbm, hbm,
            _const_spec((1, D_MODEL)),
        ],
        out_specs=pl.BlockSpec((TM_FFN, D_MODEL), lambda i: (i, 0)),
        out_shape=jax.ShapeDtypeStruct((n, D_MODEL), F32),
        scratch_shapes=[
            pltpu.VMEM((TM_FFN, D_MODEL), BF16),
            pltpu.VMEM((TM_FFN, D_FF), BF16),
            pltpu.VMEM((D_MODEL, D_FF), BF16),
            pltpu.VMEM((D_MODEL, D_FF), BF16),
            pltpu.VMEM((D_FF, D_MODEL), BF16),
            pltpu.VMEM((2, 2, D_MODEL, FF_TILE), F32),
            pltpu.VMEM((2, FF_TILE, D_MODEL), F32),
            pltpu.SemaphoreType.DMA((3, 2)),
        ],
        compiler_params=pltpu.CompilerParams(
            dimension_semantics=("arbitrary",), vmem_limit_bytes=VMEM_LIMIT_BYTES),
        name="swiglu_half_step",
    )(x, g, wg, wu, wd, fg)


def _lower_bound(lbl_ref, layer):
    logits = lbl_ref[...]
    m = jnp.max(logits, axis=0, keepdims=True)
    e = jnp.exp(logits - m)
    p = e / jnp.sum(e, axis=0, keepdims=True)
    lb = jnp.zeros((1, HGRN_W), F32)
    for j in range(1, layer + 1):
        lb = lb + p[j:j + 1, :]
    return lb


def _head(hd):
    return slice(hd * HEAD_DIM, (hd + 1) * HEAD_DIM)


def _sub_scan(lf2):
    n8 = CHUNK // 8
    w = lf2.reshape(n8, 8, HGRN_W)
    sub8 = lax.broadcasted_iota(jnp.int32, (n8, 8, HGRN_W), 1)
    for d in (1, 2, 4):
        w = w + jnp.where(sub8 >= d, pltpu.roll(w, d, axis=1), 0.0)
    w = w.reshape(CHUNK, HGRN_W)
    parts = []
    for j in range(N_SUB):
        lo = w[SUB * j:SUB * j + 8, :]
        hi = w[SUB * j + 8:SUB * j + 16, :] + lo[7:8, :]
        parts += [lo, hi]
    return jnp.concatenate(parts, axis=0)


def _hgrn_chunk_head(r0, hd, c2_ref, lk_ref, q_ref, v_ref, st_ref, e_tot, e_pre, decay):
    hs = _head(hd)
    rows = slice(r0, r0 + CHUNK)
    c2 = c2_ref[hd, rows, :]
    lk = lk_ref[hd, rows, :]
    q = q_ref[hd, rows, :]
    v_b = v_ref[hd, rows, :].astype(BF16)

    def sub(a, j):
        return a[SUB * j:SUB * (j + 1), :]

    tot = [c2[SUB * j + SUB - 1:SUB * j + SUB, :] for j in range(N_SUB)]
    tot_own = jnp.concatenate([jnp.broadcast_to(t, (SUB, HEAD_DIM)) for t in tot], axis=0)
    qh = q * jnp.exp2(c2)
    kh = jnp.exp2(lk + tot_own)
    et = [None] + [e_tot[j][:, hs] for j in range(1, N_SUB)]
    qg = jnp.concatenate([sub(qh, 0)] + [sub(qh, j) * e_pre[j][:, hs] for j in range(1, N_SUB)],
                         axis=0).astype(BF16)
    qh_b = qh.astype(BF16)

    zero = jnp.zeros((SUB, HEAD_DIM), F32)
    k2_0 = sub(kh, 0) * et[1]
    k3_0 = k2_0 * et[2]
    k3_1 = sub(kh, 1) * et[2]
    k1 = jnp.concatenate([sub(kh, 0), zero, zero, zero], axis=0).astype(BF16)
    k2 = jnp.concatenate([k2_0, sub(kh, 1), zero, zero], axis=0).astype(BF16)
    k3 = jnp.concatenate([k3_0, k3_1, sub(kh, 2), zero], axis=0).astype(BF16)
    kg = jnp.concatenate([k3_0 * et[3], k3_1 * et[3], sub(kh, 2) * et[3], sub(kh, 3)],
                         axis=0).astype(BF16)

    a1 = _dot_nt(qh_b[SUB:2 * SUB, :], k1)
    a2 = _dot_nt(qh_b[2 * SUB:3 * SUB, :], k2)
    a3 = _dot_nt(qh_b[3 * SUB:4 * SUB, :], k3)
    a_cross = jnp.concatenate([jnp.zeros((SUB, CHUNK), F32), a1, a2, a3], axis=0)

    lane = lax.broadcasted_iota(jnp.int32, (8, CHUNK), 1)
    row8 = lax.broadcasted_iota(jnp.int32, (8, CHUNK), 0)
    blocks = []
    for j in range(N_SUB):
        base = r0 + SUB * j
        c_lo, c_hi = c2[SUB * j:SUB * j + 8, :], c2[SUB * j + 8:SUB * j + 16, :]
        q_lo, q_hi = q[SUB * j:SUB * j + 8, :], q[SUB * j + 8:SUB * j + 16, :]
        d_lo = jnp.zeros((8, CHUNK), F32)
        d_hi = jnp.zeros((8, CHUNK), F32)
        for s in range(SUB):
            lks = jnp.broadcast_to(lk_ref[hd, base + s:base + s + 1, :], (8, HEAD_DIM))
            here = lane == SUB * j + s
            col_hi = jnp.sum(q_hi * jnp.exp2(c_hi + lks), axis=-1, keepdims=True)
            d_hi = jnp.where(here, col_hi, d_hi)
            if s < 8:
                col_lo = jnp.sum(q_lo * jnp.exp2(c_lo + lks), axis=-1, keepdims=True)
                d_lo = jnp.where(here, col_lo, d_lo)
        d_lo = jnp.where(lane <= SUB * j + row8, d_lo, 0.0)
        d_hi = jnp.where(lane <= SUB * j + 8 + row8, d_hi, 0.0)
        blocks += [d_lo, d_hi]
    a = (a_cross + jnp.concatenate(blocks, axis=0)).astype(BF16)
    o_intra = _dot(a, v_b)

    st = st_ref[hd]
    o_state = _dot_nt(qg, st.astype(BF16))
    st_ref[hd] = decay[:, hs] * st + _dot_tn(v_b, kg)
    return o_state + o_intra


def _mixer_kernel(x_ref, xn_ref, g_ref, win_hbm, poolw_ref, pscale_ref, lbl_ref, hnorm_ref, wpp_hbm, whp_hbm,
                  wout_hbm, o_ref, h_ref, fz_ref, qz_ref, vz_ref, c2_ref, lk_ref, q_ref, v_ref, dec_ref, ext_ref,
                  hg_ref, pm_ref, og_ref, ga_ref, gb_ref, st_ref, win_ref, wpp_ref, whp_ref, wout_ref,
                  stage_in_ref, stage_sq_ref, sem_ref, *, layer):
    tb = pl.program_id(1)
    n_chunks = TM_MIX // CHUNK

    @pl.when((pl.program_id(0) == 0) & (tb == 0))
    def _():
        _cast_weight(win_hbm, layer, win_ref, stage_in_ref, sem_ref)
        _cast_weight(wpp_hbm, layer, wpp_ref, stage_sq_ref, sem_ref)
        _cast_weight(whp_hbm, layer, whp_ref, stage_sq_ref, sem_ref)
        _cast_weight(wout_hbm, layer, wout_ref, stage_sq_ref, sem_ref)

    lb = _lower_bound(lbl_ref, layer)
    log_lb = jnp.log(lb)
    log1m_lb = jnp.log1p(-lb)

    def tile(t):
        return slice(t * FF_TILE, (t + 1) * FF_TILE)

    def norm_job(src_ref, slot):
        def run():
            h_ref[slot] = _rmsnorm(src_ref[...], g_ref[...]).astype(BF16)
        return run

    def stage_job(dst_ref, off, t, slot):
        def run():
            dst_ref[:, tile(t)] = _dot(h_ref[slot], win_ref[:, off + t * FF_TILE:off + (t + 1) * FF_TILE])
        return run

    def stage_jobs(slot):
        return [stage_job(dst_ref, off, t, slot)
                for dst_ref, off in ((fz_ref, OFF_F), (qz_ref, OFF_Q), (vz_ref, OFF_I))
                for t in range(HGRN_W // FF_TILE)]

    def gates_job(ci):
        def run():
            rows = slice(ci * CHUNK, (ci + 1) * CHUNK)
            f = fz_ref[rows, :]
            l1p = jnp.log(1.0 + jnp.exp(-jnp.abs(f)))
            bb = log1m_lb + (jnp.minimum(f, 0.0) - l1p)
            lf2 = (jnp.maximum(log_lb, bb) + jnp.log(1.0 + jnp.exp(-jnp.abs(log_lb - bb)))) * LOG2E
            lk2 = (log1m_lb - jnp.maximum(f, 0.0) - l1p) * LOG2E
            c2 = _sub_scan(lf2)
            lk = lk2 - c2
            qc = qz_ref[rows, :]
            q = qc * _sigmoid(qc)
            vc = vz_ref[rows, :]
            for hd in range(HEADS):
                c2_ref[hd, rows, :] = c2[:, _head(hd)]
                lk_ref[hd, rows, :] = lk[:, _head(hd)]
                q_ref[hd, rows, :] = q[:, _head(hd)]
                v_ref[hd, rows, :] = vc[:, _head(hd)]
            tot = [c2[SUB * j + SUB - 1:SUB * j + SUB, :] for j in range(N_SUB)]
            p1 = tot[0]
            p2 = p1 + tot[1]
            p3 = p2 + tot[2]
            dec_ref[ci] = jnp.exp2(jnp.concatenate(
                [tot[1], tot[2], tot[3], p1, p2, p3, p3 + tot[3], p3 + tot[3]], axis=0))
        return run

    @pl.when(tb == 0)
    def _():
        st_ref[...] = jnp.zeros_like(st_ref)
        ext_ref[0:POOL_CARRY, :] = jnp.zeros((POOL_CARRY, POOL_WIDTH), F32)
        for job in [norm_job(x_ref, 0)] + stage_jobs(0) + [gates_job(ci) for ci in range(n_chunks)]:
            job()

    @pl.when(tb > 0)
    def _():
        h_ref[0] = h_ref[1]

    def proj(col0):
        return _dot(h_ref[0], win_ref[:, col0:col0 + FF_TILE])

    def pool_proj_job(t):
        def run():
            ext_ref[POOL_CARRY:POOL_CARRY + TM_MIX, tile(t)] = proj(OFF_POOL + t * FF_TILE)
        return run

    def og_job(t):
        def run():
            og = proj(OFF_OG + t * FF_TILE)
            og_ref[:, tile(t)] = og * _sigmoid(og)
        return run

    def gate_job(dst_ref, off, t, act):
        def run():
            dst_ref[:, tile(t)] = act(proj(off + t * FF_TILE))
        return run

    def pool_mix_job(gi):
        def run():
            w = POOL_WINDOWS[gi]
            cols = slice(gi * POOL_GROUP_DIM, (gi + 1) * POOL_GROUP_DIM)
            pos = lax.broadcasted_iota(jnp.int32, (POOL_CARRY, POOL_GROUP_DIM), 0) + (tb * TM_MIX + 1)
            ext = ext_ref[:, cols]
            wsum = ext
            d = 1
            while d < w:
                wsum = wsum + pltpu.roll(wsum, d, axis=0)
                d *= 2
            wsum = wsum[POOL_CARRY:, :]
            u = ext[POOL_CARRY:, :]
            inv_head = 1.0 / jnp.minimum(pos, w).astype(F32)
            mean = jnp.concatenate([wsum[:POOL_CARRY] * inv_head, wsum[POOL_CARRY:] * (1.0 / w)], axis=0)
            mixed = _dot((mean - u).astype(BF16), poolw_ref[gi].astype(BF16))
            pm_ref[:, cols] = (mixed * pscale_ref[:, cols]).astype(BF16)
        return run

    def pool_merge_job():
        ga_ref[...] = ga_ref[...] * _dot(pm_ref[...], wpp_ref[...])
        ext_ref[0:POOL_CARRY, :] = ext_ref[TM_MIX:TM_MIX + POOL_CARRY, :]

    og_jobs = [og_job(t) for t in range(HGRN_W // FF_TILE)]
    pool_proj_jobs = [pool_proj_job(t) for t in range(POOL_WIDTH // FF_TILE)]
    ga_jobs = [gate_job(ga_ref, OFF_GA, t, _sigmoid) for t in range(D_MODEL // FF_TILE)]
    gb_jobs = [gate_job(gb_ref, OFF_GB, t, lambda z: z) for t in range(D_MODEL // FF_TILE)]
    mix_jobs = [pool_mix_job(gi) for gi in range(len(POOL_WINDOWS))]
    nxt = stage_jobs(1)
    schedule = [
        og_jobs + [norm_job(xn_ref, 1)] + nxt[0:2],
        nxt[2:6],
        pool_proj_jobs + ga_jobs[0:2] + [gates_job(0)],
        ga_jobs[2:4] + gb_jobs[0:2] + [gates_job(1), gates_job(2)],
        gb_jobs[2:4] + mix_jobs[0:2] + [gates_job(3)],
        mix_jobs[2:4] + [pool_merge_job, gates_job(4)],
        [gates_job(5)],
        [gates_job(6)],
    ]
    assert len(schedule) == n_chunks
    gates_after_loop = (n_chunks - 1,)

    for ci in range(n_chunks):
        for job in schedule[ci]:
            job()
        r0 = ci * CHUNK
        dec = dec_ref[ci]
        e_tot = [None] + [dec[j - 1:j, :] for j in range(1, N_SUB)]
        e_pre = [None] + [dec[N_SUB - 2 + j:N_SUB - 1 + j, :] for j in range(1, N_SUB)]
        decay = dec[2 * N_SUB - 2:2 * N_SUB - 1, :]
        outs = [_hgrn_chunk_head(r0, hd, c2_ref, lk_ref, q_ref, v_ref, st_ref, e_tot, e_pre, decay)
                for hd in range(HEADS)]
        normed = []
        for oh in outs:
            ms = jnp.mean(oh * oh, axis=-1, keepdims=True)
            normed.append(oh * lax.rsqrt(ms + EPS))
        on = jnp.concatenate(normed, axis=-1) * hnorm_ref[...]
        hg_ref[r0:r0 + CHUNK, :] = (on * og_ref[r0:r0 + CHUNK, :]).astype(BF16)

    for k in gates_after_loop:
        gates_job(k)()
    pb = _dot(hg_ref[...], whp_ref[...])
    merged = (ga_ref[...] + _sigmoid(gb_ref[...]) * pb).astype(BF16)
    o_ref[...] = x_ref[...] + _dot(merged, wout_ref[...])


def _mixer(x, g, win, poolw, pscale, lbl, hnorm, wpp, whp, wout, layer, batch):
    n = x.shape[0]
    nt = n // batch // TM_MIX
    depth = lbl.shape[0]
    hbm = pl.BlockSpec(memory_space=pl.ANY)
    return pl.pallas_call(
        functools.partial(_mixer_kernel, layer=layer),
        grid=(batch, nt),
        in_specs=[
            pl.BlockSpec((TM_MIX, D_MODEL), lambda b, t: (b * nt + t, 0)),
            pl.BlockSpec((TM_MIX, D_MODEL), lambda b, t: (b * nt + jnp.minimum(t + 1, nt - 1), 0)),
            _const_spec((1, D_MODEL)),
            hbm,
            _const_spec((len(POOL_WINDOWS), POOL_GROUP_DIM, POOL_GROUP_DIM)),
            _const_spec((1, POOL_WIDTH)),
            _const_spec((depth, HGRN_W)),
            _const_spec((1, HGRN_W)),
            hbm, hbm, hbm,
        ],
        out_specs=pl.BlockSpec((TM_MIX, D_MODEL), lambda b, t: (b * nt + t, 0)),
        out_shape=jax.ShapeDtypeStruct((n, D_MODEL), F32),
        scratch_shapes=[
            pltpu.VMEM((2, TM_MIX, D_MODEL), BF16),
            pltpu.VMEM((TM_MIX, HGRN_W), F32),
            pltpu.VMEM((TM_MIX, HGRN_W), F32),
            pltpu.VMEM((TM_MIX, HGRN_W), F32),
            pltpu.VMEM((HEADS, TM_MIX, HEAD_DIM), F32),
            pltpu.VMEM((HEADS, TM_MIX, HEAD_DIM), F32),
            pltpu.VMEM((HEADS, TM_MIX, HEAD_DIM), F32),
            pltpu.VMEM((HEADS, TM_MIX, HEAD_DIM), F32),
            pltpu.VMEM((TM_MIX // CHUNK, 8, HGRN_W), F32),
            pltpu.VMEM((POOL_CARRY + TM_MIX, POOL_WIDTH), F32),
            pltpu.VMEM((TM_MIX, HGRN_W), BF16),
            pltpu.VMEM((TM_MIX, POOL_WIDTH), BF16),
            pltpu.VMEM((TM_MIX, HGRN_W), F32),
            pltpu.VMEM((TM_MIX, D_MODEL), F32),
            pltpu.VMEM((TM_MIX, D_MODEL), F32),
            pltpu.VMEM((HEADS, HEAD_DIM, HEAD_DIM), F32),
            pltpu.VMEM((D_MODEL, D_IN), BF16),
            pltpu.VMEM((POOL_WIDTH, D_MODEL), BF16),
            pltpu.VMEM((HGRN_W, D_MODEL), BF16),
            pltpu.VMEM((D_MODEL, D_MODEL), BF16),
            pltpu.VMEM((2, CAST_ROWS_IN_PROJ, D_IN), F32),
            pltpu.VMEM((2, CAST_ROWS_SQUARE, D_MODEL), F32),
            pltpu.SemaphoreType.DMA((2,)),
        ],
        compiler_params=pltpu.CompilerParams(
            dimension_semantics=("arbitrary", "arbitrary"), vmem_limit_bytes=VMEM_LIMIT_BYTES),
        name="hybrid_mixer",
    )(x, x, g, win, poolw, pscale, lbl, hnorm, wpp, whp, wout)


def kernel(x, ffn1_norm, ffn1_w_gate, ffn1_w_up, ffn1_w_down, mix_norm, w_in, pool_w, pool_scale,
           lb_logits, hgrn_norm, w_pool_proj, w_hgrn_proj, w_out, ffn2_norm, ffn2_w_gate, ffn2_w_up,
           ffn2_w_down, final_norm):
    batch, seq, _ = x.shape
    depth = ffn1_norm.shape[0]
    xf = x.reshape(batch * seq, D_MODEL)
    fg = final_norm.reshape(1, D_MODEL)
    for l in range(depth):
        xf = _ffn(xf, ffn1_norm[l].reshape(1, D_MODEL), ffn1_w_gate, ffn1_w_up, ffn1_w_down, fg, l, False)
        xf = _mixer(xf, mix_norm[l].reshape(1, D_MODEL), w_in, pool_w[l],
                    pool_scale[l].reshape(1, POOL_WIDTH), lb_logits, hgrn_norm[l].reshape(1, HGRN_W),
                    w_pool_proj, w_hgrn_proj, w_out, l, batch)
        xf = _ffn(xf, ffn2_norm[l].reshape(1, D_MODEL), ffn2_w_gate, ffn2_w_up, ffn2_w_down, fg, l,
                  l == depth - 1)
    return xf.reshape(batch, seq, D_MODEL)
```

```python
import functools

import jax
import jax.numpy as jnp
from jax import lax
from jax.experimental import pallas as pl
from jax.experimental.pallas import tpu as pltpu

D_MODEL = 1024
D_FF = 2816
CHUNK = 64
SUB = 16
N_SUB = CHUNK // SUB
POOL_WIDTH = 512
POOL_WINDOWS = (2, 4, 8, 16)
POOL_GROUP_DIM = 128
POOL_CARRY = 16
HEADS = 4
HEAD_DIM = 128
HGRN_W = HEADS * HEAD_DIM
EPS = 1e-6
LOG2E = 1.4426950408889634

FF_TILE = 256
N_FF_TILES = D_FF // FF_TILE
TM_FFN = 512
TM_MIX = 512
CAST_ROWS_IN_PROJ = 64
CAST_ROWS_SQUARE = 256
VMEM_LIMIT_BYTES = 56 * 1024 * 1024

OFF_POOL = 0
OFF_Q = POOL_WIDTH
OFF_F = OFF_Q + HGRN_W
OFF_I = OFF_F + HGRN_W
OFF_OG = OFF_I + HGRN_W
OFF_GA = OFF_OG + HGRN_W
OFF_GB = OFF_GA + D_MODEL
D_IN = OFF_GB + D_MODEL

F32 = jnp.float32
BF16 = jnp.bfloat16


def _dot(a, b):
    return jnp.dot(a, b, preferred_element_type=F32)


def _dot_nt(a, b):
    return lax.dot_general(a, b, (((1,), (1,)), ((), ())), preferred_element_type=F32)


def _dot_tn(a, b):
    return lax.dot_general(a, b, (((0,), (0,)), ((), ())), preferred_element_type=F32)


def _rmsnorm(x, g):
    ms = jnp.mean(x * x, axis=-1, keepdims=True)
    return x * lax.rsqrt(ms + EPS) * g


def _sigmoid(x):
    return 1.0 / (1.0 + jnp.exp(-x))


def _weight_chunk_copy(src_hbm, layer, stage_ref, sem_ref, c):
    rows = stage_ref.shape[1]
    return pltpu.make_async_copy(src_hbm.at[layer, pl.ds(c * rows, rows), :], stage_ref.at[c % 2],
                                 sem_ref.at[c % 2])


def _cast_weight(src_hbm, layer, dst_ref, stage_ref, sem_ref):
    rows = stage_ref.shape[1]
    n_chunks, rem = divmod(dst_ref.shape[-2], rows)
    assert rem == 0
    _weight_chunk_copy(src_hbm, layer, stage_ref, sem_ref, 0).start()
    for c in range(n_chunks):
        if c + 1 < n_chunks:
            _weight_chunk_copy(src_hbm, layer, stage_ref, sem_ref, c + 1).start()
        _weight_chunk_copy(src_hbm, layer, stage_ref, sem_ref, c).wait()
        if len(dst_ref.shape) == 2:
            dst_ref[c * rows:(c + 1) * rows, :] = stage_ref[c % 2].astype(BF16)
        else:
            width = dst_ref.shape[2]
            for t in range(dst_ref.shape[0]):
                dst_ref[t, c * rows:(c + 1) * rows, :] = stage_ref[c % 2, :, t * width:(t + 1) * width].astype(BF16)


def _ffn_kernel(x_ref, g_ref, wg_hbm, wu_hbm, wd_hbm, fg_ref, o_ref, h_ref, a_ref, wg_ref, wu_ref, wd_ref,
                stage_in_ref, stage_out_ref, sem_ref, *, layer, final_norm):
    def tile_copies(j):
        k = j % 2
        cols = pl.ds(j * FF_TILE, FF_TILE)
        return (pltpu.make_async_copy(wg_hbm.at[layer, :, cols], stage_in_ref.at[0, k], sem_ref.at[0, k]),
                pltpu.make_async_copy(wu_hbm.at[layer, :, cols], stage_in_ref.at[1, k], sem_ref.at[1, k]),
                pltpu.make_async_copy(wd_hbm.at[layer, cols, :], stage_out_ref.at[k], sem_ref.at[2, k]))

    def fetch_tile(j):
        for copy in tile_copies(j):
            copy.start()

    def convert_tile(j):
        cols = slice(j * FF_TILE, (j + 1) * FF_TILE)
        if j + 1 < N_FF_TILES:
            fetch_tile(j + 1)
        for copy in tile_copies(j):
            copy.wait()
        wg_ref[:, cols] = stage_in_ref[0, j % 2].astype(BF16)
        wu_ref[:, cols] = stage_in_ref[1, j % 2].astype(BF16)
        wd_ref[cols, :] = stage_out_ref[j % 2].astype(BF16)

    def half_step(before_tile):
        x = x_ref[...]
        h_ref[...] = _rmsnorm(x, g_ref[...]).astype(BF16)
        for j in range(N_FF_TILES):
            before_tile(j)
            cols = slice(j * FF_TILE, (j + 1) * FF_TILE)
            h = h_ref[...]
            g = _dot(h, wg_ref[:, cols])
            u = _dot(h, wu_ref[:, cols])
            a_ref[:, cols] = (g * _sigmoid(g) * u).astype(BF16)
        y = x + 0.5 * _dot(a_ref[...], wd_ref[...])
        if final_norm:
            y = _rmsnorm(y, fg_ref[...])
        o_ref[...] = y

    @pl.when(pl.program_id(0) == 0)
    def _():
        fetch_tile(0)
        half_step(convert_tile)

    @pl.when(pl.program_id(0) > 0)
    def _():
        half_step(lambda j: None)


def _const_spec(shape):
    nd = len(shape)
    return pl.BlockSpec(shape, lambda *_: (0,) * nd, pipeline_mode=pl.Buffered(1))


def _ffn(x, g, wg, wu, wd, fg, layer, final_norm):
    n = x.shape[0]
    hbm = pl.BlockSpec(memory_space=pl.ANY)
    return pl.pallas_call(
        functools.partial(_ffn_kernel, layer=layer, final_norm=final_norm),
        grid=(n // TM_FFN,),
        in_specs=[
            pl.BlockSpec((TM_FFN, D_MODEL), lambda i: (i, 0)),
            _const_spec((1, D_MODEL)),
            hbm, hbm, hbm,
            _const_spec((1, D_MODEL)),
        ],
        out_specs=pl.BlockSpec((TM_FFN, D_MODEL), lambda i: (i, 0)),
        out_shape=jax.ShapeDtypeStruct((n, D_MODEL), F32),
        scratch_shapes=[
            pltpu.VMEM((TM_FFN, D_MODEL), BF16),
            pltpu.VMEM((TM_FFN, D_FF), BF16),
            pltpu.VMEM((D_MODEL, D_FF), BF16),
            pltpu.VMEM((D_MODEL, D_FF), BF16),
            pltpu.VMEM((D_FF, D_MODEL), BF16),
            pltpu.VMEM((2, 2, D_MODEL, FF_TILE), F32),
            pltpu.VMEM((2, FF_TILE, D_MODEL), F32),
            pltpu.SemaphoreType.DMA((3, 2)),
        ],
        compiler_params=pltpu.CompilerParams(
            dimension_semantics=("arbitrary",), vmem_limit_bytes=VMEM_LIMIT_BYTES),
        name="swiglu_half_step",
    )(x, g, wg, wu, wd, fg)


def _lower_bound(lbl_ref, layer):
    logits = lbl_ref[...]
    m = jnp.max(logits, axis=0, keepdims=True)
    e = jnp.exp(logits - m)
    p = e / jnp.sum(e, axis=0, keepdims=True)
    lb = jnp.zeros((1, HGRN_W), F32)
    for j in range(1, layer + 1):
        lb = lb + p[j:j + 1, :]
    return lb


def _head(hd):
    return slice(hd * HEAD_DIM, (hd + 1) * HEAD_DIM)


def _sub_scan(lf2):
    n8 = CHUNK // 8
    w = lf2.reshape(n8, 8, HGRN_W)
    sub8 = lax.broadcasted_iota(jnp.int32, (n8, 8, HGRN_W), 1)
    for d in (1, 2, 4):
        w = w + jnp.where(sub8 >= d, pltpu.roll(w, d, axis=1), 0.0)
    w = w.reshape(CHUNK, HGRN_W)
    parts = []
    for j in range(N_SUB):
        lo = w[SUB * j:SUB * j + 8, :]
        hi = w[SUB * j + 8:SUB * j + 16, :] + lo[7:8, :]
        parts += [lo, hi]
    return jnp.concatenate(parts, axis=0)


def _hgrn_chunk_head(r0, hd, c2_ref, lk_ref, q_ref, v_ref, st_ref, e_tot, e_pre, decay):
    hs = _head(hd)
    rows = slice(r0, r0 + CHUNK)
    c2 = c2_ref[hd, rows, :]
    lk = lk_ref[hd, rows, :]
    q = q_ref[hd, rows, :]
    v_b = v_ref[hd, rows, :].astype(BF16)

    def sub(a, j):
        return a[SUB * j:SUB * (j + 1), :]

    tot = [c2[SUB * j + SUB - 1:SUB * j + SUB, :] for j in range(N_SUB)]
    tot_own = jnp.concatenate([jnp.broadcast_to(t, (SUB, HEAD_DIM)) for t in tot], axis=0)
    qh = q * jnp.exp2(c2)
    kh = jnp.exp2(lk + tot_own)
    et = [None] + [e_tot[j][:, hs] for j in range(1, N_SUB)]
    qg = jnp.concatenate([sub(qh, 0)] + [sub(qh, j) * e_pre[j][:, hs] for j in range(1, N_SUB)],
                         axis=0).astype(BF16)
    qh_b = qh.astype(BF16)

    zero = jnp.zeros((SUB, HEAD_DIM), F32)
    k2_0 = sub(kh, 0) * et[1]
    k3_0 = k2_0 * et[2]
    k3_1 = sub(kh, 1) * et[2]
    k1 = jnp.concatenate([sub(kh, 0), zero, zero, zero], axis=0).astype(BF16)
    k2 = jnp.concatenate([k2_0, sub(kh, 1), zero, zero], axis=0).astype(BF16)
    k3 = jnp.concatenate([k3_0, k3_1, sub(kh, 2), zero], axis=0).astype(BF16)
    kg = jnp.concatenate([k3_0 * et[3], k3_1 * et[3], sub(kh, 2) * et[3], sub(kh, 3)],
                         axis=0).astype(BF16)

    a1 = _dot_nt(qh_b[SUB:2 * SUB, :], k1)
    a2 = _dot_nt(qh_b[2 * SUB:3 * SUB, :], k2)
    a3 = _dot_nt(qh_b[3 * SUB:4 * SUB, :], k3)
    a_cross = jnp.concatenate([jnp.zeros((SUB, CHUNK), F32), a1, a2, a3], axis=0)

    lane = lax.broadcasted_iota(jnp.int32, (8, CHUNK), 1)
    row8 = lax.broadcasted_iota(jnp.int32, (8, CHUNK), 0)
    blocks = []
    for j in range(N_SUB):
        base = r0 + SUB * j
        c_lo, c_hi = c2[SUB * j:SUB * j + 8, :], c2[SUB * j + 8:SUB * j + 16, :]
        q_lo, q_hi = q[SUB * j:SUB * j + 8, :], q[SUB * j + 8:SUB * j + 16, :]
        d_lo = jnp.zeros((8, CHUNK), F32)
        d_hi = jnp.zeros((8, CHUNK), F32)
        for s in range(SUB):
            lks = jnp.broadcast_to(lk_ref[hd, base + s:base + s + 1, :], (8, HEAD_DIM))
            here = lane == SUB * j + s
            col_hi = jnp.sum(q_hi * jnp.exp2(c_hi + lks), axis=-1, keepdims=True)
            d_hi = jnp.where(here, col_hi, d_hi)
            if s < 8:
                col_lo = jnp.sum(q_lo * jnp.exp2(c_lo + lks), axis=-1, keepdims=True)
                d_lo = jnp.where(here, col_lo, d_lo)
        d_lo = jnp.where(lane <= SUB * j + row8, d_lo, 0.0)
        d_hi = jnp.where(lane <= SUB * j + 8 + row8, d_hi, 0.0)
        blocks += [d_lo, d_hi]
    a = (a_cross + jnp.concatenate(blocks, axis=0)).astype(BF16)
    o_intra = _dot(a, v_b)

    st = st_ref[hd]
    o_state = _dot_nt(qg, st.astype(BF16))
    st_ref[hd] = decay[:, hs] * st + _dot_tn(v_b, kg)
    return o_state + o_intra


def _mixer_kernel(x_ref, xn_ref, g_ref, win_hbm, poolw_ref, pscale_ref, lbl_ref, hnorm_ref, wpp_hbm, whp_hbm,
                  wout_hbm, o_ref, h_ref, fz_ref, qz_ref, vz_ref, c2_ref, lk_ref, q_ref, v_ref, dec_ref, ext_ref,
                  hg_ref, pm_ref, og_ref, ga_ref, gb_ref, st_ref, win_ref, wpp_ref, whp_ref, wout_ref,
                  stage_in_ref, stage_sq_ref, sem_ref, *, layer):
    tb = pl.program_id(1)
    n_chunks = TM_MIX // CHUNK

    @pl.when((pl.program_id(0) == 0) & (tb == 0))
    def _():
        _cast_weight(win_hbm, layer, win_ref, stage_in_ref, sem_ref)
        _cast_weight(wpp_hbm, layer, wpp_ref, stage_sq_ref, sem_ref)
        _cast_weight(whp_hbm, layer, whp_ref, stage_sq_ref, sem_ref)
        _cast_weight(wout_hbm, layer, wout_ref, stage_sq_ref, sem_ref)

    lb = _lower_bound(lbl_ref, layer)
    log_lb = jnp.log(lb)
    log1m_lb = jnp.log1p(-lb)

    def tile(t):
        return slice(t * FF_TILE, (t + 1) * FF_TILE)

    def norm_job(src_ref, slot):
        def run():
            h_ref[slot] = _rmsnorm(src_ref[...], g_ref[...]).astype(BF16)
        return run

    def stage_job(dst_ref, off, t, slot):
        def run():
            dst_ref[:, tile(t)] = _dot(h_ref[slot], win_ref[off // FF_TILE + t])
        return run

    def stage_jobs(slot):
        return [stage_job(dst_ref, off, t, slot)
                for dst_ref, off in ((fz_ref, OFF_F), (qz_ref, OFF_Q), (vz_ref, OFF_I))
                for t in range(HGRN_W // FF_TILE)]

    def gates_job(ci):
        def run():
            rows = slice(ci * CHUNK, (ci + 1) * CHUNK)
            f = fz_ref[rows, :]
            l1p = jnp.log(1.0 + jnp.exp(-jnp.abs(f)))
            bb = log1m_lb + (jnp.minimum(f, 0.0) - l1p)
            lf2 = (jnp.maximum(log_lb, bb) + jnp.log(1.0 + jnp.exp(-jnp.abs(log_lb - bb)))) * LOG2E
            lk2 = (log1m_lb - jnp.maximum(f, 0.0) - l1p) * LOG2E
            c2 = _sub_scan(lf2)
            lk = lk2 - c2
            qc = qz_ref[rows, :]
            q = qc * _sigmoid(qc)
            vc = vz_ref[rows, :]
            for hd in range(HEADS):
                c2_ref[hd, rows, :] = c2[:, _head(hd)]
                lk_ref[hd, rows, :] = lk[:, _head(hd)]
                q_ref[hd, rows, :] = q[:, _head(hd)]
                v_ref[hd, rows, :] = vc[:, _head(hd)]
            tot = [c2[SUB * j + SUB - 1:SUB * j + SUB, :] for j in range(N_SUB)]
            p1 = tot[0]
            p2 = p1 + tot[1]
            p3 = p2 + tot[2]
            dec_ref[ci] = jnp.exp2(jnp.concatenate(
                [tot[1], tot[2], tot[3], p1, p2, p3, p3 + tot[3], p3 + tot[3]], axis=0))
        return run

    @pl.when(tb == 0)
    def _():
        st_ref[...] = jnp.zeros_like(st_ref)
        ext_ref[0:POOL_CARRY, :] = jnp.zeros((POOL_CARRY, POOL_WIDTH), F32)
        for job in [norm_job(x_ref, 0)] + stage_jobs(0) + [gates_job(ci) for ci in range(n_chunks)]:
            job()

    @pl.when(tb > 0)
    def _():
        h_ref[0] = h_ref[1]

    def proj(col0):
        return _dot(h_ref[0], win_ref[col0 // FF_TILE])

    def pool_proj_job(t):
        def run():
            ext_ref[POOL_CARRY:POOL_CARRY + TM_MIX, tile(t)] = proj(OFF_POOL + t * FF_TILE)
        return run

    def og_job(t):
        def run():
            og = proj(OFF_OG + t * FF_TILE)
            og_ref[:, tile(t)] = og * _sigmoid(og)
        return run

    def gate_job(dst_ref, off, t, act):
        def run():
            dst_ref[:, tile(t)] = act(proj(off + t * FF_TILE))
        return run

    def pool_mix_job(gi):
        def run():
            w = POOL_WINDOWS[gi]
            cols = slice(gi * POOL_GROUP_DIM, (gi + 1) * POOL_GROUP_DIM)
            pos = lax.broadcasted_iota(jnp.int32, (POOL_CARRY, POOL_GROUP_DIM), 0) + (tb * TM_MIX + 1)
            ext = ext_ref[:, cols]
            wsum = ext
            d = 1
            while d < w:
                wsum = wsum + pltpu.roll(wsum, d, axis=0)
                d *= 2
            wsum = wsum[POOL_CARRY:, :]
            u = ext[POOL_CARRY:, :]
            inv_head = 1.0 / jnp.minimum(pos, w).astype(F32)
            mean = jnp.concatenate([wsum[:POOL_CARRY] * inv_head, wsum[POOL_CARRY:] * (1.0 / w)], axis=0)
            mixed = _dot((mean - u).astype(BF16), poolw_ref[gi].astype(BF16))
            pm_ref[:, cols] = (mixed * pscale_ref[:, cols]).astype(BF16)
        return run

    def pool_merge_job():
        ga_ref[...] = ga_ref[...] * _dot(pm_ref[...], wpp_ref[...])
        ext_ref[0:POOL_CARRY, :] = ext_ref[TM_MIX:TM_MIX + POOL_CARRY, :]

    og_jobs = [og_job(t) for t in range(HGRN_W // FF_TILE)]
    pool_proj_jobs = [pool_proj_job(t) for t in range(POOL_WIDTH // FF_TILE)]
    ga_jobs = [gate_job(ga_ref, OFF_GA, t, _sigmoid) for t in range(D_MODEL // FF_TILE)]
    gb_jobs = [gate_job(gb_ref, OFF_GB, t, lambda z: z) for t in range(D_MODEL // FF_TILE)]
    mix_jobs = [pool_mix_job(gi) for gi in range(len(POOL_WINDOWS))]
    nxt = stage_jobs(1)
    schedule = [
        og_jobs + [norm_job(xn_ref, 1)] + nxt[0:1],
        nxt[1:4],
        nxt[4:6] + pool_proj_jobs[0:1],
        pool_proj_jobs[1:2] + ga_jobs[0:2] + [gates_job(0)],
        ga_jobs[2:4] + gb_jobs[0:1] + [gates_job(1), gates_job(2)],
        gb_jobs[1:3] + mix_jobs[0:2] + [gates_job(3)],
        gb_jobs[3:4] + mix_jobs[2:4] + [pool_merge_job, gates_job(4)],
        [gates_job(5)],
    ]
    assert len(schedule) == n_chunks
    gates_after_loop = (n_chunks - 2, n_chunks - 1)

    for ci in range(n_chunks):
        for job in schedule[ci]:
            job()
        r0 = ci * CHUNK
        dec = dec_ref[ci]
        e_tot = [None] + [dec[j - 1:j, :] for j in range(1, N_SUB)]
        e_pre = [None] + [dec[N_SUB - 2 + j:N_SUB - 1 + j, :] for j in range(1, N_SUB)]
        decay = dec[2 * N_SUB - 2:2 * N_SUB - 1, :]
        outs = [_hgrn_chunk_head(r0, hd, c2_ref, lk_ref, q_ref, v_ref, st_ref, e_tot, e_pre, decay)
                for hd in range(HEADS)]
        normed = []
        for oh in outs:
            ms = jnp.mean(oh * oh, axis=-1, keepdims=True)
            normed.append(oh * lax.rsqrt(ms + EPS))
        on = jnp.concatenate(normed, axis=-1) * hnorm_ref[...]
        hg_ref[r0:r0 + CHUNK, :] = (on * og_ref[r0:r0 + CHUNK, :]).astype(BF16)

    for k in gates_after_loop:
        gates_job(k)()
    pb = _dot(hg_ref[...], whp_ref[...])
    merged = (ga_ref[...] + _sigmoid(gb_ref[...]) * pb).astype(BF16)
    o_ref[...] = x_ref[...] + _dot(merged, wout_ref[...])


def _mixer(x, g, win, poolw, pscale, lbl, hnorm, wpp, whp, wout, layer, batch):
    n = x.shape[0]
    nt = n // batch // TM_MIX
    depth = lbl.shape[0]
    hbm = pl.BlockSpec(memory_space=pl.ANY)
    return pl.pallas_call(
        functools.partial(_mixer_kernel, layer=layer),
        grid=(batch, nt),
        in_specs=[
            pl.BlockSpec((TM_MIX, D_MODEL), lambda b, t: (b * nt + t, 0)),
            pl.BlockSpec((TM_MIX, D_MODEL), lambda b, t: (b * nt + jnp.minimum(t + 1, nt - 1), 0)),
            _const_spec((1, D_MODEL)),
            hbm,
            _const_spec((len(POOL_WINDOWS), POOL_GROUP_DIM, POOL_GROUP_DIM)),
            _const_spec((1, POOL_WIDTH)),
            _const_spec((depth, HGRN_W)),
            _const_spec((1, HGRN_W)),
            hbm, hbm, hbm,
        ],
        out_specs=pl.BlockSpec((TM_MIX, D_MODEL), lambda b, t: (b * nt + t, 0)),
        out_shape=jax.ShapeDtypeStruct((n, D_MODEL), F32),
        scratch_shapes=[
            pltpu.VMEM((2, TM_MIX, D_MODEL), BF16),
            pltpu.VMEM((TM_MIX, HGRN_W), F32),
            pltpu.VMEM((TM_MIX, HGRN_W), F32),
            pltpu.VMEM((TM_MIX, HGRN_W), F32),
            pltpu.VMEM((HEADS, TM_MIX, HEAD_DIM), F32),
            pltpu.VMEM((HEADS, TM_MIX, HEAD_DIM), F32),
            pltpu.VMEM((HEADS, TM_MIX, HEAD_DIM), F32),
            pltpu.VMEM((HEADS, TM_MIX, HEAD_DIM), F32),
            pltpu.VMEM((TM_MIX // CHUNK, 8, HGRN_W), F32),
            pltpu.VMEM((POOL_CARRY + TM_MIX, POOL_WIDTH), F32),
            pltpu.VMEM((TM_MIX, HGRN_W), BF16),
            pltpu.VMEM((TM_MIX, POOL_WIDTH), BF16),
            pltpu.VMEM((TM_MIX, HGRN_W), F32),
            pltpu.VMEM((TM_MIX, D_MODEL), F32),
            pltpu.VMEM((TM_MIX, D_MODEL), F32),
            pltpu.VMEM((HEADS, HEAD_DIM, HEAD_DIM), F32),
            pltpu.VMEM((D_IN // FF_TILE, D_MODEL, FF_TILE), BF16),
            pltpu.VMEM((POOL_WIDTH, D_MODEL), BF16),
            pltpu.VMEM((HGRN_W, D_MODEL), BF16),
            pltpu.VMEM((D_MODEL, D_MODEL), BF16),
            pltpu.VMEM((2, CAST_ROWS_IN_PROJ, D_IN), F32),
            pltpu.VMEM((2, CAST_ROWS_SQUARE, D_MODEL), F32),
            pltpu.SemaphoreType.DMA((2,)),
        ],
        compiler_params=pltpu.CompilerParams(
            dimension_semantics=("arbitrary", "arbitrary"), vmem_limit_bytes=VMEM_LIMIT_BYTES),
        name="hybrid_mixer",
    )(x, x, g, win, poolw, pscale, lbl, hnorm, wpp, whp, wout)


def kernel(x, ffn1_norm, ffn1_w_gate, ffn1_w_up, ffn1_w_down, mix_norm, w_in, pool_w, pool_scale,
           lb_logits, hgrn_norm, w_pool_proj, w_hgrn_proj, w_out, ffn2_norm, ffn2_w_gate, ffn2_w_up,
           ffn2_w_down, final_norm):
    batch, seq, _ = x.shape
    depth = ffn1_norm.shape[0]
    xf = x.reshape(batch * seq, D_MODEL)
    fg = final_norm.reshape(1, D_MODEL)
    for l in range(depth):
        xf = _ffn(xf, ffn1_norm[l].reshape(1, D_MODEL), ffn1_w_gate, ffn1_w_up, ffn1_w_down, fg, l, False)
        xf = _mixer(xf, mix_norm[l].reshape(1, D_MODEL), w_in, pool_w[l],
                    pool_scale[l].reshape(1, POOL_WIDTH), lb_logits, hgrn_norm[l].reshape(1, HGRN_W),
                    w_pool_proj, w_hgrn_proj, w_out, l, batch)
        xf = _ffn(xf, ffn2_norm[l].reshape(1, D_MODEL), ffn2_w_gate, ffn2_w_up, ffn2_w_down, fg, l,
                  l == depth - 1)
    return xf.reshape(batch, seq, D_MODEL)
```

```python
import functools

import jax
import jax.numpy as jnp
from jax import lax
from jax.experimental import pallas as pl
from jax.experimental.pallas import tpu as pltpu

D_MODEL = 1024
D_FF = 2816
CHUNK = 64
SUB = 16
N_SUB = CHUNK // SUB
POOL_WIDTH = 512
POOL_WINDOWS = (2, 4, 8, 16)
POOL_GROUP_DIM = 128
POOL_CARRY = 16
HEADS = 4
HEAD_DIM = 128
HGRN_W = HEADS * HEAD_DIM
EPS = 1e-6
LOG2E = 1.4426950408889634

FF_TILE = 256
N_FF_TILES = D_FF // FF_TILE
TM_FFN = 512
TM_MIX = 512
CAST_ROWS_IN_PROJ = 64
CAST_ROWS_SQUARE = 256
VMEM_LIMIT_BYTES = 56 * 1024 * 1024

OFF_POOL = 0
OFF_Q = POOL_WIDTH
OFF_F = OFF_Q + HGRN_W
OFF_I = OFF_F + HGRN_W
OFF_OG = OFF_I + HGRN_W
OFF_GA = OFF_OG + HGRN_W
OFF_GB = OFF_GA + D_MODEL
D_IN = OFF_GB + D_MODEL

F32 = jnp.float32
BF16 = jnp.bfloat16


def _dot(a, b):
    return jnp.dot(a, b, preferred_element_type=F32)


def _dot_nt(a, b):
    return lax.dot_general(a, b, (((1,), (1,)), ((), ())), preferred_element_type=F32)


def _dot_tn(a, b):
    return lax.dot_general(a, b, (((0,), (0,)), ((), ())), preferred_element_type=F32)


def _rmsnorm(x, g):
    ms = jnp.mean(x * x, axis=-1, keepdims=True)
    return x * lax.rsqrt(ms + EPS) * g


def _sigmoid(x):
    return 1.0 / (1.0 + jnp.exp(-x))


def _weight_chunk_copy(src_hbm, layer, stage_ref, sem_ref, c):
    rows = stage_ref.shape[1]
    return pltpu.make_async_copy(src_hbm.at[layer, pl.ds(c * rows, rows), :], stage_ref.at[c % 2],
                                 sem_ref.at[c % 2])


def _cast_weight(src_hbm, layer, dst_ref, stage_ref, sem_ref):
    rows = stage_ref.shape[1]
    n_chunks, rem = divmod(dst_ref.shape[-2], rows)
    assert rem == 0
    _weight_chunk_copy(src_hbm, layer, stage_ref, sem_ref, 0).start()
    for c in range(n_chunks):
        if c + 1 < n_chunks:
            _weight_chunk_copy(src_hbm, layer, stage_ref, sem_ref, c + 1).start()
        _weight_chunk_copy(src_hbm, layer, stage_ref, sem_ref, c).wait()
        if len(dst_ref.shape) == 2:
            dst_ref[c * rows:(c + 1) * rows, :] = stage_ref[c % 2].astype(BF16)
        else:
            width = dst_ref.shape[2]
            for t in range(dst_ref.shape[0]):
                dst_ref[t, c * rows:(c + 1) * rows, :] = stage_ref[c % 2, :, t * width:(t + 1) * width].astype(BF16)


def _ffn_kernel(x_ref, g_ref, wg_hbm, wu_hbm, wd_hbm, fg_ref, o_ref, h_ref, a_ref, wg_ref, wu_ref, wd_ref,
                stage_in_ref, stage_out_ref, sem_ref, *, layer, final_norm):
    def tile_copies(j):
        k = j % 2
        cols = pl.ds(j * FF_TILE, FF_TILE)
        return (pltpu.make_async_copy(wg_hbm.at[layer, :, cols], stage_in_ref.at[0, k], sem_ref.at[0, k]),
                pltpu.make_async_copy(wu_hbm.at[layer, :, cols], stage_in_ref.at[1, k], sem_ref.at[1, k]),
                pltpu.make_async_copy(wd_hbm.at[layer, cols, :], stage_out_ref.at[k], sem_ref.at[2, k]))

    def fetch_tile(j):
        for copy in tile_copies(j):
            copy.start()

    def convert_tile(j):
        cols = slice(j * FF_TILE, (j + 1) * FF_TILE)
        if j + 1 < N_FF_TILES:
            fetch_tile(j + 1)
        for copy in tile_copies(j):
            copy.wait()
        wg_ref[:, cols] = stage_in_ref[0, j % 2].astype(BF16)
        wu_ref[:, cols] = stage_in_ref[1, j % 2].astype(BF16)
        wd_ref[cols, :] = stage_out_ref[j % 2].astype(BF16)

    def half_step(before_tile):
        x = x_ref[...]
        h_ref[...] = _rmsnorm(x, g_ref[...]).astype(BF16)
        for j in range(N_FF_TILES):
            before_tile(j)
            cols = slice(j * FF_TILE, (j + 1) * FF_TILE)
            h = h_ref[...]
            g = _dot(h, wg_ref[:, cols])
            u = _dot(h, wu_ref[:, cols])
            a_ref[:, cols] = (g * _sigmoid(g) * u).astype(BF16)
        y = x + 0.5 * _dot(a_ref[...], wd_ref[...])
        if final_norm:
            y = _rmsnorm(y, fg_ref[...])
        o_ref[...] = y

    @pl.when(pl.program_id(0) == 0)
    def _():
        fetch_tile(0)
        half_step(convert_tile)

    @pl.when(pl.program_id(0) > 0)
    def _():
        half_step(lambda j: None)


def _const_spec(shape):
    nd = len(shape)
    return pl.BlockSpec(shape, lambda *_: (0,) * nd, pipeline_mode=pl.Buffered(1))


def _ffn(x, g, wg, wu, wd, fg, layer, final_norm):
    n = x.shape[0]
    hbm = pl.BlockSpec(memory_space=pl.ANY)
    return pl.pallas_call(
        functools.partial(_ffn_kernel, layer=layer, final_norm=final_norm),
        grid=(n // TM_FFN,),
        in_specs=[
            pl.BlockSpec((TM_FFN, D_MODEL), lambda i: (i, 0)),
            _const_spec((1, D_MODEL)),
            hbm, hbm, hbm,
            _const_spec((1, D_MODEL)),
        ],
        out_specs=pl.BlockSpec((TM_FFN, D_MODEL), lambda i: (i, 0)),
        out_shape=jax.ShapeDtypeStruct((n, D_MODEL), F32),
        scratch_shapes=[
            pltpu.VMEM((TM_FFN, D_MODEL), BF16),
            pltpu.VMEM((TM_FFN, D_FF), BF16),
            pltpu.VMEM((D_MODEL, D_FF), BF16),
            pltpu.VMEM((D_MODEL, D_FF), BF16),
            pltpu.VMEM((D_FF, D_MODEL), BF16),
            pltpu.VMEM((2, 2, D_MODEL, FF_TILE), F32),
            pltpu.VMEM((2, FF_TILE, D_MODEL), F32),
            pltpu.SemaphoreType.DMA((3, 2)),
        ],
        compiler_params=pltpu.CompilerParams(
            dimension_semantics=("arbitrary",), vmem_limit_bytes=VMEM_LIMIT_BYTES),
        name="swiglu_half_step",
    )(x, g, wg, wu, wd, fg)


def _lower_bound(lbl_ref, layer):
    logits = lbl_ref[...]
    m = jnp.max(logits, axis=0, keepdims=True)
    e = jnp.exp(logits - m)
    p = e / jnp.sum(e, axis=0, keepdims=True)
    lb = jnp.zeros((1, HGRN_W), F32)
    for j in range(1, layer + 1):
        lb = lb + p[j:j + 1, :]
    return lb


def _head(hd):
    return slice(hd * HEAD_DIM, (hd + 1) * HEAD_DIM)


def _sub_scan(lf2):
    n8 = CHUNK // 8
    w = lf2.reshape(n8, 8, HGRN_W)
    sub8 = lax.broadcasted_iota(jnp.int32, (n8, 8, HGRN_W), 1)
    for d in (1, 2, 4):
        w = w + jnp.where(sub8 >= d, pltpu.roll(w, d, axis=1), 0.0)
    w = w.reshape(CHUNK, HGRN_W)
    parts = []
    for j in range(N_SUB):
        lo = w[SUB * j:SUB * j + 8, :]
        hi = w[SUB * j + 8:SUB * j + 16, :] + lo[7:8, :]
        parts += [lo, hi]
    return jnp.concatenate(parts, axis=0)


def _hgrn_chunk_head(r0, hd, c2_ref, lk_ref, q_ref, v_ref, st_ref, e_tot, e_pre, decay):
    hs = _head(hd)
    rows = slice(r0, r0 + CHUNK)
    c2 = c2_ref[hd, rows, :]
    lk = lk_ref[hd, rows, :]
    q = q_ref[hd, rows, :]
    v_b = v_ref[hd, rows, :].astype(BF16)

    def sub(a, j):
        return a[SUB * j:SUB * (j + 1), :]

    tot = [c2[SUB * j + SUB - 1:SUB * j + SUB, :] for j in range(N_SUB)]
    tot_own = jnp.concatenate([jnp.broadcast_to(t, (SUB, HEAD_DIM)) for t in tot], axis=0)
    qh = q * jnp.exp2(c2)
    kh = jnp.exp2(lk + tot_own)
    et = [None] + [e_tot[j][:, hs] for j in range(1, N_SUB)]
    qg = jnp.concatenate([sub(qh, 0)] + [sub(qh, j) * e_pre[j][:, hs] for j in range(1, N_SUB)],
                         axis=0).astype(BF16)

    e23 = et[2] * et[3]
    kg = jnp.concatenate([sub(kh, 0) * (et[1] * e23), sub(kh, 1) * e23, sub(kh, 2) * et[3], sub(kh, 3)],
                         axis=0).astype(BF16)

    e12 = et[1] * et[2]
    lhs = jnp.concatenate([sub(qh, 1), sub(qh, 2) * et[1], sub(qh, 3) * e12,
                           sub(qh, 2), sub(qh, 3) * et[2],
                           sub(qh, 3)], axis=0).astype(BF16)
    r = _dot(lhs, kh.T.astype(BF16))
    lane16 = lax.broadcasted_iota(jnp.int32, (SUB, CHUNK), 1)
    a1 = jnp.where(lane16 < SUB, r[0:SUB], 0.0)
    a2 = jnp.where(lane16 < SUB, r[SUB:2 * SUB], jnp.where(lane16 < 2 * SUB, r[3 * SUB:4 * SUB], 0.0))
    a3 = jnp.where(lane16 < SUB, r[2 * SUB:3 * SUB],
                   jnp.where(lane16 < 2 * SUB, r[4 * SUB:5 * SUB],
                             jnp.where(lane16 < 3 * SUB, r[5 * SUB:6 * SUB], 0.0)))
    a_cross = jnp.concatenate([jnp.zeros((SUB, CHUNK), F32), a1, a2, a3], axis=0)

    lane = lax.broadcasted_iota(jnp.int32, (8, CHUNK), 1)
    row8 = lax.broadcasted_iota(jnp.int32, (8, CHUNK), 0)
    blocks = []
    for j in range(N_SUB):
        base = r0 + SUB * j
        c_lo, c_hi = c2[SUB * j:SUB * j + 8, :], c2[SUB * j + 8:SUB * j + 16, :]
        q_lo, q_hi = q[SUB * j:SUB * j + 8, :], q[SUB * j + 8:SUB * j + 16, :]
        d_lo = jnp.zeros((8, CHUNK), F32)
        d_hi = jnp.zeros((8, CHUNK), F32)
        for s in range(SUB):
            lks = jnp.broadcast_to(lk_ref[hd, base + s:base + s + 1, :], (8, HEAD_DIM))
            here = lane == SUB * j + s
            col_hi = jnp.sum(q_hi * jnp.exp2(c_hi + lks), axis=-1, keepdims=True)
            d_hi = jnp.where(here, col_hi, d_hi)
            if s < 8:
                col_lo = jnp.sum(q_lo * jnp.exp2(c_lo + lks), axis=-1, keepdims=True)
                d_lo = jnp.where(here, col_lo, d_lo)
        d_lo = jnp.where(lane <= SUB * j + row8, d_lo, 0.0)
        d_hi = jnp.where(lane <= SUB * j + 8 + row8, d_hi, 0.0)
        blocks += [d_lo, d_hi]
    a = (a_cross + jnp.concatenate(blocks, axis=0)).astype(BF16)
    o_intra = _dot(a, v_b)

    st = st_ref[hd]
    o_state = _dot_nt(qg, st.astype(BF16))
    st_ref[hd] = decay[:, hs] * st + _dot_tn(v_b, kg)
    return o_state + o_intra


def _mixer_kernel(x_ref, xn_ref, g_ref, win_hbm, poolw_ref, pscale_ref, lbl_ref, hnorm_ref, wpp_hbm, whp_hbm,
                  wout_hbm, o_ref, h_ref, fz_ref, qz_ref, vz_ref, c2_ref, lk_ref, q_ref, v_ref, dec_ref, ext_ref,
                  hg_ref, pm_ref, og_ref, ga_ref, gb_ref, st_ref, win_ref, wpp_ref, whp_ref, wout_ref,
                  stage_in_ref, stage_sq_ref, sem_ref, *, layer):
    tb = pl.program_id(1)
    n_chunks = TM_MIX // CHUNK

    @pl.when((pl.program_id(0) == 0) & (tb == 0))
    def _():
        _cast_weight(win_hbm, layer, win_ref, stage_in_ref, sem_ref)
        _cast_weight(wpp_hbm, layer, wpp_ref, stage_sq_ref, sem_ref)
        _cast_weight(whp_hbm, layer, whp_ref, stage_sq_ref, sem_ref)
        _cast_weight(wout_hbm, layer, wout_ref, stage_sq_ref, sem_ref)

    lb = _lower_bound(lbl_ref, layer)
    log_lb = jnp.log(lb)
    log1m_lb = jnp.log1p(-lb)

    def tile(t):
        return slice(t * FF_TILE, (t + 1) * FF_TILE)

    def norm_job(src_ref, slot):
        def run():
            h_ref[slot] = _rmsnorm(src_ref[...], g_ref[...]).astype(BF16)
        return run

    def stage_job(dst_ref, off, t, slot):
        def run():
            dst_ref[:, tile(t)] = _dot(h_ref[slot], win_ref[off // FF_TILE + t])
        return run

    def stage_jobs(slot):
        return [stage_job(dst_ref, off, t, slot)
                for dst_ref, off in ((fz_ref, OFF_F), (qz_ref, OFF_Q), (vz_ref, OFF_I))
                for t in range(HGRN_W // FF_TILE)]

    def gates_job(ci):
        def run():
            rows = slice(ci * CHUNK, (ci + 1) * CHUNK)
            f = fz_ref[rows, :]
            l1p = jnp.log(1.0 + jnp.exp(-jnp.abs(f)))
            bb = log1m_lb + (jnp.minimum(f, 0.0) - l1p)
            lf2 = (jnp.maximum(log_lb, bb) + jnp.log(1.0 + jnp.exp(-jnp.abs(log_lb - bb)))) * LOG2E
            lk2 = (log1m_lb - jnp.maximum(f, 0.0) - l1p) * LOG2E
            c2 = _sub_scan(lf2)
            lk = lk2 - c2
            qc = qz_ref[rows, :]
            q = qc * _sigmoid(qc)
            vc = vz_ref[rows, :]
            for hd in range(HEADS):
                c2_ref[hd, rows, :] = c2[:, _head(hd)]
                lk_ref[hd, rows, :] = lk[:, _head(hd)]
                q_ref[hd, rows, :] = q[:, _head(hd)]
                v_ref[hd, rows, :] = vc[:, _head(hd)]
            tot = [c2[SUB * j + SUB - 1:SUB * j + SUB, :] for j in range(N_SUB)]
            p1 = tot[0]
            p2 = p1 + tot[1]
            p3 = p2 + tot[2]
            dec_ref[ci] = jnp.exp2(jnp.concatenate(
                [tot[1], tot[2], tot[3], p1, p2, p3, p3 + tot[3], p3 + tot[3]], axis=0))
        return run

    @pl.when(tb == 0)
    def _():
        st_ref[...] = jnp.zeros_like(st_ref)
        ext_ref[0:POOL_CARRY, :] = jnp.zeros((POOL_CARRY, POOL_WIDTH), F32)
        for job in [norm_job(x_ref, 0)] + stage_jobs(0) + [gates_job(ci) for ci in range(n_chunks)]:
            job()

    @pl.when(tb > 0)
    def _():
        h_ref[0] = h_ref[1]

    def proj(col0):
        return _dot(h_ref[0], win_ref[col0 // FF_TILE])

    def pool_proj_job(t):
        def run():
            ext_ref[POOL_CARRY:POOL_CARRY + TM_MIX, tile(t)] = proj(OFF_POOL + t * FF_TILE)
        return run

    def og_job(t):
        def run():
            og = proj(OFF_OG + t * FF_TILE)
            og_ref[:, tile(t)] = og * _sigmoid(og)
        return run

    def gate_job(dst_ref, off, t, act):
        def run():
            dst_ref[:, tile(t)] = act(proj(off + t * FF_TILE))
        return run

    def pool_mix_job(gi):
        def run():
            w = POOL_WINDOWS[gi]
            cols = slice(gi * POOL_GROUP_DIM, (gi + 1) * POOL_GROUP_DIM)
            pos = lax.broadcasted_iota(jnp.int32, (POOL_CARRY, POOL_GROUP_DIM), 0) + (tb * TM_MIX + 1)
            ext = ext_ref[:, cols]
            wsum = ext
            d = 1
            while d < w:
                wsum = wsum + pltpu.roll(wsum, d, axis=0)
                d *= 2
            wsum = wsum[POOL_CARRY:, :]
            u = ext[POOL_CARRY:, :]
            inv_head = 1.0 / jnp.minimum(pos, w).astype(F32)
            mean = jnp.concatenate([wsum[:POOL_CARRY] * inv_head, wsum[POOL_CARRY:] * (1.0 / w)], axis=0)
            mixed = _dot((mean - u).astype(BF16), poolw_ref[gi].astype(BF16))
            pm_ref[:, cols] = (mixed * pscale_ref[:, cols]).astype(BF16)
        return run

    def pool_merge_job():
        ga_ref[...] = ga_ref[...] * _dot(pm_ref[...], wpp_ref[...])
        ext_ref[0:POOL_CARRY, :] = ext_ref[TM_MIX:TM_MIX + POOL_CARRY, :]

    og_jobs = [og_job(t) for t in range(HGRN_W // FF_TILE)]
    pool_proj_jobs = [pool_proj_job(t) for t in range(POOL_WIDTH // FF_TILE)]
    ga_jobs = [gate_job(ga_ref, OFF_GA, t, _sigmoid) for t in range(D_MODEL // FF_TILE)]
    gb_jobs = [gate_job(gb_ref, OFF_GB, t, lambda z: z) for t in range(D_MODEL // FF_TILE)]
    mix_jobs = [pool_mix_job(gi) for gi in range(len(POOL_WINDOWS))]
    nxt = stage_jobs(1)
    schedule = [
        og_jobs + [norm_job(xn_ref, 1)] + nxt[0:1],
        nxt[1:4],
        nxt[4:6] + pool_proj_jobs[0:1],
        pool_proj_jobs[1:2] + ga_jobs[0:2] + [gates_job(0)],
        ga_jobs[2:4] + gb_jobs[0:1] + [gates_job(1), gates_job(2)],
        gb_jobs[1:3] + mix_jobs[0:2] + [gates_job(3)],
        gb_jobs[3:4] + mix_jobs[2:4] + [pool_merge_job, gates_job(4)],
        [gates_job(5)],
    ]
    assert len(schedule) == n_chunks
    gates_after_loop = (n_chunks - 2, n_chunks - 1)

    for ci in range(n_chunks):
        for job in schedule[ci]:
            job()
        r0 = ci * CHUNK
        dec = dec_ref[ci]
        e_tot = [None] + [dec[j - 1:j, :] for j in range(1, N_SUB)]
        e_pre = [None] + [dec[N_SUB - 2 + j:N_SUB - 1 + j, :] for j in range(1, N_SUB)]
        decay = dec[2 * N_SUB - 2:2 * N_SUB - 1, :]
        outs = [_hgrn_chunk_head(r0, hd, c2_ref, lk_ref, q_ref, v_ref, st_ref, e_tot, e_pre, decay)
                for hd in range(HEADS)]
        normed = []
        for oh in outs:
            ms = jnp.mean(oh * oh, axis=-1, keepdims=True)
            normed.append(oh * lax.rsqrt(ms + EPS))
        on = jnp.concatenate(normed, axis=-1) * hnorm_ref[...]
        hg_ref[r0:r0 + CHUNK, :] = (on * og_ref[r0:r0 + CHUNK, :]).astype(BF16)

    for k in gates_after_loop:
        gates_job(k)()
    pb = _dot(hg_ref[...], whp_ref[...])
    merged = (ga_ref[...] + _sigmoid(gb_ref[...]) * pb).astype(BF16)
    o_ref[...] = x_ref[...] + _dot(merged, wout_ref[...])


def _mixer(x, g, win, poolw, pscale, lbl, hnorm, wpp, whp, wout, layer, batch):
    n = x.shape[0]
    nt = n // batch // TM_MIX
    depth = lbl.shape[0]
    hbm = pl.BlockSpec(memory_space=pl.ANY)
    return pl.pallas_call(
        functools.partial(_mixer_kernel, layer=layer),
        grid=(batch, nt),
        in_specs=[
            pl.BlockSpec((TM_MIX, D_MODEL), lambda b, t: (b * nt + t, 0)),
            pl.BlockSpec((TM_MIX, D_MODEL), lambda b, t: (b * nt + jnp.minimum(t + 1, nt - 1), 0)),
            _const_spec((1, D_MODEL)),
            hbm,
            _const_spec((len(POOL_WINDOWS), POOL_GROUP_DIM, POOL_GROUP_DIM)),
            _const_spec((1, POOL_WIDTH)),
            _const_spec((depth, HGRN_W)),
            _const_spec((1, HGRN_W)),
            hbm, hbm, hbm,
        ],
        out_specs=pl.BlockSpec((TM_MIX, D_MODEL), lambda b, t: (b * nt + t, 0)),
        out_shape=jax.ShapeDtypeStruct((n, D_MODEL), F32),
        scratch_shapes=[
            pltpu.VMEM((2, TM_MIX, D_MODEL), BF16),
            pltpu.VMEM((TM_MIX, HGRN_W), F32),
            pltpu.VMEM((TM_MIX, HGRN_W), F32),
            pltpu.VMEM((TM_MIX, HGRN_W), F32),
            pltpu.VMEM((HEADS, TM_MIX, HEAD_DIM), F32),
            pltpu.VMEM((HEADS, TM_MIX, HEAD_DIM), F32),
            pltpu.VMEM((HEADS, TM_MIX, HEAD_DIM), F32),
            pltpu.VMEM((HEADS, TM_MIX, HEAD_DIM), F32),
            pltpu.VMEM((TM_MIX // CHUNK, 8, HGRN_W), F32),
            pltpu.VMEM((POOL_CARRY + TM_MIX, POOL_WIDTH), F32),
            pltpu.VMEM((TM_MIX, HGRN_W), BF16),
            pltpu.VMEM((TM_MIX, POOL_WIDTH), BF16),
            pltpu.VMEM((TM_MIX, HGRN_W), F32),
            pltpu.VMEM((TM_MIX, D_MODEL), F32),
            pltpu.VMEM((TM_MIX, D_MODEL), F32),
            pltpu.VMEM((HEADS, HEAD_DIM, HEAD_DIM), F32),
            pltpu.VMEM((D_IN // FF_TILE, D_MODEL, FF_TILE), BF16),
            pltpu.VMEM((POOL_WIDTH, D_MODEL), BF16),
            pltpu.VMEM((HGRN_W, D_MODEL), BF16),
            pltpu.VMEM((D_MODEL, D_MODEL), BF16),
            pltpu.VMEM((2, CAST_ROWS_IN_PROJ, D_IN), F32),
            pltpu.VMEM((2, CAST_ROWS_SQUARE, D_MODEL), F32),
            pltpu.SemaphoreType.DMA((2,)),
        ],
        compiler_params=pltpu.CompilerParams(
            dimension_semantics=("arbitrary", "arbitrary"), vmem_limit_bytes=VMEM_LIMIT_BYTES),
        name="hybrid_mixer",
    )(x, x, g, win, poolw, pscale, lbl, hnorm, wpp, whp, wout)


def kernel(x, ffn1_norm, ffn1_w_gate, ffn1_w_up, ffn1_w_down, mix_norm, w_in, pool_w, pool_scale,
           lb_logits, hgrn_norm, w_pool_proj, w_hgrn_proj, w_out, ffn2_norm, ffn2_w_gate, ffn2_w_up,
           ffn2_w_down, final_norm):
    batch, seq, _ = x.shape
    depth = ffn1_norm.shape[0]
    xf = x.reshape(batch * seq, D_MODEL)
    fg = final_norm.reshape(1, D_MODEL)
    for l in range(depth):
        xf = _ffn(xf, ffn1_norm[l].reshape(1, D_MODEL), ffn1_w_gate, ffn1_w_up, ffn1_w_down, fg, l, False)
        xf = _mixer(xf, mix_norm[l].reshape(1, D_MODEL), w_in, pool_w[l],
                    pool_scale[l].reshape(1, POOL_WIDTH), lb_logits, hgrn_norm[l].reshape(1, HGRN_W),
                    w_pool_proj, w_hgrn_proj, w_out, l, batch)
        xf = _ffn(xf, ffn2_norm[l].reshape(1, D_MODEL), ffn2_w_gate, ffn2_w_up, ffn2_w_down, fg, l,
                  l == depth - 1)
    return xf.reshape(batch, seq, D_MODEL)
```

```python
import functools

import jax
import jax.numpy as jnp
from jax import lax
from jax.experimental import pallas as pl
from jax.experimental.pallas import tpu as pltpu

D_MODEL = 1024
D_FF = 2816
CHUNK = 64
SUB = 16
N_SUB = CHUNK // SUB
POOL_WIDTH = 512
POOL_WINDOWS = (2, 4, 8, 16)
POOL_GROUP_DIM = 128
POOL_CARRY = 16
HEADS = 4
HEAD_DIM = 128
HGRN_W = HEADS * HEAD_DIM
EPS = 1e-6
LOG2E = 1.4426950408889634

FF_TILE = 256
N_FF_TILES = D_FF // FF_TILE
TM_FFN = 512
TM_MIX = 512
CAST_ROWS_IN_PROJ = 64
CAST_ROWS_SQUARE = 256
VMEM_LIMIT_BYTES = 56 * 1024 * 1024

OFF_POOL = 0
OFF_Q = POOL_WIDTH
OFF_F = OFF_Q + HGRN_W
OFF_I = OFF_F + HGRN_W
OFF_OG = OFF_I + HGRN_W
OFF_GA = OFF_OG + HGRN_W
OFF_GB = OFF_GA + D_MODEL
D_IN = OFF_GB + D_MODEL

F32 = jnp.float32
BF16 = jnp.bfloat16


def _dot(a, b):
    return jnp.dot(a, b, preferred_element_type=F32)


def _dot_nt(a, b):
    return lax.dot_general(a, b, (((1,), (1,)), ((), ())), preferred_element_type=F32)


def _dot_tn(a, b):
    return lax.dot_general(a, b, (((0,), (0,)), ((), ())), preferred_element_type=F32)


def _rmsnorm(x, g):
    ms = jnp.mean(x * x, axis=-1, keepdims=True)
    return x * lax.rsqrt(ms + EPS) * g


def _sigmoid(x):
    return 1.0 / (1.0 + jnp.exp(-x))


def _weight_chunk_copy(src_hbm, layer, stage_ref, sem_ref, c):
    rows = stage_ref.shape[1]
    return pltpu.make_async_copy(src_hbm.at[layer, pl.ds(c * rows, rows), :], stage_ref.at[c % 2],
                                 sem_ref.at[c % 2])


def _cast_weight(src_hbm, layer, dst_ref, stage_ref, sem_ref):
    rows = stage_ref.shape[1]
    n_chunks, rem = divmod(dst_ref.shape[-2], rows)
    assert rem == 0
    _weight_chunk_copy(src_hbm, layer, stage_ref, sem_ref, 0).start()
    for c in range(n_chunks):
        if c + 1 < n_chunks:
            _weight_chunk_copy(src_hbm, layer, stage_ref, sem_ref, c + 1).start()
        _weight_chunk_copy(src_hbm, layer, stage_ref, sem_ref, c).wait()
        if len(dst_ref.shape) == 2:
            dst_ref[c * rows:(c + 1) * rows, :] = stage_ref[c % 2].astype(BF16)
        else:
            width = dst_ref.shape[2]
            for t in range(dst_ref.shape[0]):
                dst_ref[t, c * rows:(c + 1) * rows, :] = stage_ref[c % 2, :, t * width:(t + 1) * width].astype(BF16)


def _ffn_kernel(x_ref, g_ref, wg_hbm, wu_hbm, wd_hbm, fg_ref, o_ref, h_ref, a_ref, wg_ref, wu_ref, wd_ref,
                stage_in_ref, stage_out_ref, sem_ref, *, layer, final_norm):
    def tile_copies(j):
        k = j % 2
        cols = pl.ds(j * FF_TILE, FF_TILE)
        return (pltpu.make_async_copy(wg_hbm.at[layer, :, cols], stage_in_ref.at[0, k], sem_ref.at[0, k]),
                pltpu.make_async_copy(wu_hbm.at[layer, :, cols], stage_in_ref.at[1, k], sem_ref.at[1, k]),
                pltpu.make_async_copy(wd_hbm.at[layer, cols, :], stage_out_ref.at[k], sem_ref.at[2, k]))

    def fetch_tile(j):
        for copy in tile_copies(j):
            copy.start()

    def convert_tile(j):
        cols = slice(j * FF_TILE, (j + 1) * FF_TILE)
        if j + 1 < N_FF_TILES:
            fetch_tile(j + 1)
        for copy in tile_copies(j):
            copy.wait()
        wg_ref[:, cols] = stage_in_ref[0, j % 2].astype(BF16)
        wu_ref[:, cols] = stage_in_ref[1, j % 2].astype(BF16)
        wd_ref[cols, :] = stage_out_ref[j % 2].astype(BF16)

    def half_step(before_tile):
        x = x_ref[...]
        h_ref[...] = _rmsnorm(x, g_ref[...]).astype(BF16)
        for j in range(N_FF_TILES):
            before_tile(j)
            cols = slice(j * FF_TILE, (j + 1) * FF_TILE)
            h = h_ref[...]
            g = _dot(h, wg_ref[:, cols])
            u = _dot(h, wu_ref[:, cols])
            a_ref[:, cols] = (g * _sigmoid(g) * u).astype(BF16)
        y = x + 0.5 * _dot(a_ref[...], wd_ref[...])
        if final_norm:
            y = _rmsnorm(y, fg_ref[...])
        o_ref[...] = y

    @pl.when(pl.program_id(0) == 0)
    def _():
        fetch_tile(0)
        half_step(convert_tile)

    @pl.when(pl.program_id(0) > 0)
    def _():
        half_step(lambda j: None)


def _const_spec(shape):
    nd = len(shape)
    return pl.BlockSpec(shape, lambda *_: (0,) * nd, pipeline_mode=pl.Buffered(1))


def _ffn(x, g, wg, wu, wd, fg, layer, final_norm):
    n = x.shape[0]
    hbm = pl.BlockSpec(memory_space=pl.ANY)
    return pl.pallas_call(
        functools.partial(_ffn_kernel, layer=layer, final_norm=final_norm),
        grid=(n // TM_FFN,),
        in_specs=[
            pl.BlockSpec((TM_FFN, D_MODEL), lambda i: (i, 0)),
            _const_spec((1, D_MODEL)),
            hbm, hbm, hbm,
            _const_spec((1, D_MODEL)),
        ],
        out_specs=pl.BlockSpec((TM_FFN, D_MODEL), lambda i: (i, 0)),
        out_shape=jax.ShapeDtypeStruct((n, D_MODEL), F32),
        scratch_shapes=[
            pltpu.VMEM((TM_FFN, D_MODEL), BF16),
            pltpu.VMEM((TM_FFN, D_FF), BF16),
            pltpu.VMEM((D_MODEL, D_FF), BF16),
            pltpu.VMEM((D_MODEL, D_FF), BF16),
            pltpu.VMEM((D_FF, D_MODEL), BF16),
            pltpu.VMEM((2, 2, D_MODEL, FF_TILE), F32),
            pltpu.VMEM((2, FF_TILE, D_MODEL), F32),
            pltpu.SemaphoreType.DMA((3, 2)),
        ],
        compiler_params=pltpu.CompilerParams(
            dimension_semantics=("arbitrary",), vmem_limit_bytes=VMEM_LIMIT_BYTES),
        name="swiglu_half_step",
    )(x, g, wg, wu, wd, fg)


def _lower_bound(lbl_ref, layer):
    logits = lbl_ref[...]
    m = jnp.max(logits, axis=0, keepdims=True)
    e = jnp.exp(logits - m)
    p = e / jnp.sum(e, axis=0, keepdims=True)
    lb = jnp.zeros((1, HGRN_W), F32)
    for j in range(1, layer + 1):
        lb = lb + p[j:j + 1, :]
    return lb


def _head(hd):
    return slice(hd * HEAD_DIM, (hd + 1) * HEAD_DIM)


def _sub_scan(lf2):
    n8 = CHUNK // 8
    w = lf2.reshape(n8, 8, HGRN_W)
    sub8 = lax.broadcasted_iota(jnp.int32, (n8, 8, HGRN_W), 1)
    for d in (1, 2, 4):
        w = w + jnp.where(sub8 >= d, pltpu.roll(w, d, axis=1), 0.0)
    w = w.reshape(CHUNK, HGRN_W)
    parts = []
    for j in range(N_SUB):
        lo = w[SUB * j:SUB * j + 8, :]
        hi = w[SUB * j + 8:SUB * j + 16, :] + lo[7:8, :]
        parts += [lo, hi]
    return jnp.concatenate(parts, axis=0)


def _hgrn_chunk_head(r0, hd, c2_ref, lk_ref, q_ref, v_ref, st_ref, e_tot, e_pre, decay):
    hs = _head(hd)
    rows = slice(r0, r0 + CHUNK)
    c2 = c2_ref[hd, rows, :]
    lk = lk_ref[hd, rows, :]
    q = q_ref[hd, rows, :]
    v_b = v_ref[hd, rows, :].astype(BF16)

    def sub(a, j):
        return a[SUB * j:SUB * (j + 1), :]

    tot = [c2[SUB * j + SUB - 1:SUB * j + SUB, :] for j in range(N_SUB)]
    tot_own = jnp.concatenate([jnp.broadcast_to(t, (SUB, HEAD_DIM)) for t in tot], axis=0)
    qh = q * jnp.exp2(c2)
    kh = jnp.exp2(lk + tot_own)
    et = [None] + [e_tot[j][:, hs] for j in range(1, N_SUB)]
    qg = jnp.concatenate([sub(qh, 0)] + [sub(qh, j) * e_pre[j][:, hs] for j in range(1, N_SUB)],
                         axis=0).astype(BF16)
    qh_b = qh.astype(BF16)

    zero = jnp.zeros((SUB, HEAD_DIM), F32)
    k2_0 = sub(kh, 0) * et[1]
    k3_0 = k2_0 * et[2]
    k3_1 = sub(kh, 1) * et[2]
    k1 = jnp.concatenate([sub(kh, 0), zero, zero, zero], axis=0).astype(BF16)
    k2 = jnp.concatenate([k2_0, sub(kh, 1), zero, zero], axis=0).astype(BF16)
    k3 = jnp.concatenate([k3_0, k3_1, sub(kh, 2), zero], axis=0).astype(BF16)
    kg = jnp.concatenate([k3_0 * et[3], k3_1 * et[3], sub(kh, 2) * et[3], sub(kh, 3)],
                         axis=0).astype(BF16)

    a1 = _dot_nt(qh_b[SUB:2 * SUB, :], k1)
    a2 = _dot_nt(qh_b[2 * SUB:3 * SUB, :], k2)
    a3 = _dot_nt(qh_b[3 * SUB:4 * SUB, :], k3)
    a_cross = jnp.concatenate([jnp.zeros((SUB, CHUNK), F32), a1, a2, a3], axis=0)
    return hs, c2, q, v_b, qg, kg, a_cross


def _hgrn_chunk_head_finish(r0, hd, lk_ref, st_ref, decay, prepared):
    hs, c2, q, v_b, qg, kg, a_cross = prepared

    lane = lax.broadcasted_iota(jnp.int32, (8, CHUNK), 1)
    row8 = lax.broadcasted_iota(jnp.int32, (8, CHUNK), 0)
    blocks = []
    for j in range(N_SUB):
        base = r0 + SUB * j
        c_lo, c_hi = c2[SUB * j:SUB * j + 8, :], c2[SUB * j + 8:SUB * j + 16, :]
        q_lo, q_hi = q[SUB * j:SUB * j + 8, :], q[SUB * j + 8:SUB * j + 16, :]
        d_lo = jnp.zeros((8, CHUNK), F32)
        d_hi = jnp.zeros((8, CHUNK), F32)
        for s in range(SUB):
            lks = jnp.broadcast_to(lk_ref[hd, base + s:base + s + 1, :], (8, HEAD_DIM))
            here = lane == SUB * j + s
            col_hi = jnp.sum(q_hi * jnp.exp2(c_hi + lks), axis=-1, keepdims=True)
            d_hi = jnp.where(here, col_hi, d_hi)
            if s < 8:
                col_lo = jnp.sum(q_lo * jnp.exp2(c_lo + lks), axis=-1, keepdims=True)
                d_lo = jnp.where(here, col_lo, d_lo)
        d_lo = jnp.where(lane <= SUB * j + row8, d_lo, 0.0)
        d_hi = jnp.where(lane <= SUB * j + 8 + row8, d_hi, 0.0)
        blocks += [d_lo, d_hi]
    a = (a_cross + jnp.concatenate(blocks, axis=0)).astype(BF16)
    o_intra = _dot(a, v_b)

    st = st_ref[hd]
    o_state = _dot_nt(qg, st.astype(BF16))
    st_ref[hd] = decay[:, hs] * st + _dot_tn(v_b, kg)
    return o_state + o_intra


def _mixer_kernel(x_ref, xn_ref, g_ref, win_hbm, poolw_ref, pscale_ref, lbl_ref, hnorm_ref, wpp_hbm, whp_hbm,
                  wout_hbm, o_ref, h_ref, fz_ref, qz_ref, vz_ref, c2_ref, lk_ref, q_ref, v_ref, dec_ref, ext_ref,
                  hg_ref, pm_ref, og_ref, ga_ref, gb_ref, st_ref, win_ref, wpp_ref, whp_ref, wout_ref,
                  stage_in_ref, stage_sq_ref, sem_ref, *, layer):
    tb = pl.program_id(1)
    n_chunks = TM_MIX // CHUNK

    @pl.when((pl.program_id(0) == 0) & (tb == 0))
    def _():
        _cast_weight(win_hbm, layer, win_ref, stage_in_ref, sem_ref)
        _cast_weight(wpp_hbm, layer, wpp_ref, stage_sq_ref, sem_ref)
        _cast_weight(whp_hbm, layer, whp_ref, stage_sq_ref, sem_ref)
        _cast_weight(wout_hbm, layer, wout_ref, stage_sq_ref, sem_ref)

    lb = _lower_bound(lbl_ref, layer)
    log_lb = jnp.log(lb)
    log1m_lb = jnp.log1p(-lb)

    def tile(t):
        return slice(t * FF_TILE, (t + 1) * FF_TILE)

    def norm_job(src_ref, slot):
        def run():
            h_ref[slot] = _rmsnorm(src_ref[...], g_ref[...]).astype(BF16)
        return run

    def stage_job(dst_ref, off, t, slot):
        def run():
            dst_ref[:, tile(t)] = _dot(h_ref[slot], win_ref[off // FF_TILE + t])
        return run

    def stage_jobs(slot):
        return [stage_job(dst_ref, off, t, slot)
                for dst_ref, off in ((fz_ref, OFF_F), (qz_ref, OFF_Q), (vz_ref, OFF_I))
                for t in range(HGRN_W // FF_TILE)]

    def gates_job(ci):
        def run():
            rows = slice(ci * CHUNK, (ci + 1) * CHUNK)
            f = fz_ref[rows, :]
            l1p = jnp.log(1.0 + jnp.exp(-jnp.abs(f)))
            bb = log1m_lb + (jnp.minimum(f, 0.0) - l1p)
            lf2 = (jnp.maximum(log_lb, bb) + jnp.log(1.0 + jnp.exp(-jnp.abs(log_lb - bb)))) * LOG2E
            lk2 = (log1m_lb - jnp.maximum(f, 0.0) - l1p) * LOG2E
            c2 = _sub_scan(lf2)
            lk = lk2 - c2
            qc = qz_ref[rows, :]
            q = qc * _sigmoid(qc)
            vc = vz_ref[rows, :]
            for hd in range(HEADS):
                c2_ref[hd, rows, :] = c2[:, _head(hd)]
                lk_ref[hd, rows, :] = lk[:, _head(hd)]
                q_ref[hd, rows, :] = q[:, _head(hd)]
                v_ref[hd, rows, :] = vc[:, _head(hd)]
            tot = [c2[SUB * j + SUB - 1:SUB * j + SUB, :] for j in range(N_SUB)]
            p1 = tot[0]
            p2 = p1 + tot[1]
            p3 = p2 + tot[2]
            dec_ref[ci] = jnp.exp2(jnp.concatenate(
                [tot[1], tot[2], tot[3], p1, p2, p3, p3 + tot[3], p3 + tot[3]], axis=0))
        return run

    @pl.when(tb == 0)
    def _():
        st_ref[...] = jnp.zeros_like(st_ref)
        ext_ref[0:POOL_CARRY, :] = jnp.zeros((POOL_CARRY, POOL_WIDTH), F32)
        for job in [norm_job(x_ref, 0)] + stage_jobs(0) + [gates_job(ci) for ci in range(n_chunks)]:
            job()

    @pl.when(tb > 0)
    def _():
        h_ref[0] = h_ref[1]

    def proj(col0):
        return _dot(h_ref[0], win_ref[col0 // FF_TILE])

    def pool_proj_job(t):
        def run():
            ext_ref[POOL_CARRY:POOL_CARRY + TM_MIX, tile(t)] = proj(OFF_POOL + t * FF_TILE)
        return run

    def og_job(t):
        def run():
            og = proj(OFF_OG + t * FF_TILE)
            og_ref[:, tile(t)] = og * _sigmoid(og)
        return run

    def gate_job(dst_ref, off, t, act):
        def run():
            dst_ref[:, tile(t)] = act(proj(off + t * FF_TILE))
        return run

    def pool_mix_job(gi):
        def run():
            w = POOL_WINDOWS[gi]
            cols = slice(gi * POOL_GROUP_DIM, (gi + 1) * POOL_GROUP_DIM)
            pos = lax.broadcasted_iota(jnp.int32, (POOL_CARRY, POOL_GROUP_DIM), 0) + (tb * TM_MIX + 1)
            ext = ext_ref[:, cols]
            wsum = ext
            d = 1
            while d < w:
                wsum = wsum + pltpu.roll(wsum, d, axis=0)
                d *= 2
            wsum = wsum[POOL_CARRY:, :]
            u = ext[POOL_CARRY:, :]
            inv_head = 1.0 / jnp.minimum(pos, w).astype(F32)
            mean = jnp.concatenate([wsum[:POOL_CARRY] * inv_head, wsum[POOL_CARRY:] * (1.0 / w)], axis=0)
            mixed = _dot((mean - u).astype(BF16), poolw_ref[gi].astype(BF16))
            pm_ref[:, cols] = (mixed * pscale_ref[:, cols]).astype(BF16)
        return run

    def pool_merge_job():
        ga_ref[...] = ga_ref[...] * _dot(pm_ref[...], wpp_ref[...])
        ext_ref[0:POOL_CARRY, :] = ext_ref[TM_MIX:TM_MIX + POOL_CARRY, :]

    og_jobs = [og_job(t) for t in range(HGRN_W // FF_TILE)]
    pool_proj_jobs = [pool_proj_job(t) for t in range(POOL_WIDTH // FF_TILE)]
    ga_jobs = [gate_job(ga_ref, OFF_GA, t, _sigmoid) for t in range(D_MODEL // FF_TILE)]
    gb_jobs = [gate_job(gb_ref, OFF_GB, t, lambda z: z) for t in range(D_MODEL // FF_TILE)]
    mix_jobs = [pool_mix_job(gi) for gi in range(len(POOL_WINDOWS))]
    nxt = stage_jobs(1)
    schedule = [
        og_jobs + [norm_job(xn_ref, 1)] + nxt[0:1],
        nxt[1:4],
        nxt[4:6] + pool_proj_jobs[0:1],
        pool_proj_jobs[1:2] + ga_jobs[0:2] + [gates_job(0)],
        ga_jobs[2:4] + gb_jobs[0:1] + [gates_job(1), gates_job(2)],
        gb_jobs[1:3] + mix_jobs[0:2] + [gates_job(3)],
        gb_jobs[3:4] + mix_jobs[2:4] + [pool_merge_job, gates_job(4)],
        [gates_job(5)],
    ]
    assert len(schedule) == n_chunks
    gates_after_loop = (n_chunks - 2, n_chunks - 1)

    for ci in range(n_chunks):
        for job in schedule[ci]:
            job()
        r0 = ci * CHUNK
        dec = dec_ref[ci]
        e_tot = [None] + [dec[j - 1:j, :] for j in range(1, N_SUB)]
        e_pre = [None] + [dec[N_SUB - 2 + j:N_SUB - 1 + j, :] for j in range(1, N_SUB)]
        decay = dec[2 * N_SUB - 2:2 * N_SUB - 1, :]
        prepared = [_hgrn_chunk_head(r0, hd, c2_ref, lk_ref, q_ref, v_ref, st_ref, e_tot, e_pre, decay)
                    for hd in range(HEADS)]
        outs = [_hgrn_chunk_head_finish(r0, hd, lk_ref, st_ref, decay, prepared[hd]) for hd in range(HEADS)]
        normed = []
        for oh in outs:
            ms = jnp.mean(oh * oh, axis=-1, keepdims=True)
            normed.append(oh * lax.rsqrt(ms + EPS))
        on = jnp.concatenate(normed, axis=-1) * hnorm_ref[...]
        hg_ref[r0:r0 + CHUNK, :] = (on * og_ref[r0:r0 + CHUNK, :]).astype(BF16)

    for k in gates_after_loop:
        gates_job(k)()
    pb = _dot(hg_ref[...], whp_ref[...])
    merged = (ga_ref[...] + _sigmoid(gb_ref[...]) * pb).astype(BF16)
    o_ref[...] = x_ref[...] + _dot(merged, wout_ref[...])


def _mixer(x, g, win, poolw, pscale, lbl, hnorm, wpp, whp, wout, layer, batch):
    n = x.shape[0]
    nt = n // batch // TM_MIX
    depth = lbl.shape[0]
    hbm = pl.BlockSpec(memory_space=pl.ANY)
    return pl.pallas_call(
        functools.partial(_mixer_kernel, layer=layer),
        grid=(batch, nt),
        in_specs=[
            pl.BlockSpec((TM_MIX, D_MODEL), lambda b, t: (b * nt + t, 0)),
            pl.BlockSpec((TM_MIX, D_MODEL), lambda b, t: (b * nt + jnp.minimum(t + 1, nt - 1), 0)),
            _const_spec((1, D_MODEL)),
            hbm,
            _const_spec((len(POOL_WINDOWS), POOL_GROUP_DIM, POOL_GROUP_DIM)),
            _const_spec((1, POOL_WIDTH)),
            _const_spec((depth, HGRN_W)),
            _const_spec((1, HGRN_W)),
            hbm, hbm, hbm,
        ],
        out_specs=pl.BlockSpec((TM_MIX, D_MODEL), lambda b, t: (b * nt + t, 0)),
        out_shape=jax.ShapeDtypeStruct((n, D_MODEL), F32),
        scratch_shapes=[
            pltpu.VMEM((2, TM_MIX, D_MODEL), BF16),
            pltpu.VMEM((TM_MIX, HGRN_W), F32),
            pltpu.VMEM((TM_MIX, HGRN_W), F32),
            pltpu.VMEM((TM_MIX, HGRN_W), F32),
            pltpu.VMEM((HEADS, TM_MIX, HEAD_DIM), F32),
            pltpu.VMEM((HEADS, TM_MIX, HEAD_DIM), F32),
            pltpu.VMEM((HEADS, TM_MIX, HEAD_DIM), F32),
            pltpu.VMEM((HEADS, TM_MIX, HEAD_DIM), F32),
            pltpu.VMEM((TM_MIX // CHUNK, 8, HGRN_W), F32),
            pltpu.VMEM((POOL_CARRY + TM_MIX, POOL_WIDTH), F32),
            pltpu.VMEM((TM_MIX, HGRN_W), BF16),
            pltpu.VMEM((TM_MIX, POOL_WIDTH), BF16),
            pltpu.VMEM((TM_MIX, HGRN_W), F32),
            pltpu.VMEM((TM_MIX, D_MODEL), F32),
            pltpu.VMEM((TM_MIX, D_MODEL), F32),
            pltpu.VMEM((HEADS, HEAD_DIM, HEAD_DIM), F32),
            pltpu.VMEM((D_IN // FF_TILE, D_MODEL, FF_TILE), BF16),
            pltpu.VMEM((POOL_WIDTH, D_MODEL), BF16),
            pltpu.VMEM((HGRN_W, D_MODEL), BF16),
            pltpu.VMEM((D_MODEL, D_MODEL), BF16),
            pltpu.VMEM((2, CAST_ROWS_IN_PROJ, D_IN), F32),
            pltpu.VMEM((2, CAST_ROWS_SQUARE, D_MODEL), F32),
            pltpu.SemaphoreType.DMA((2,)),
        ],
        compiler_params=pltpu.CompilerParams(
            dimension_semantics=("arbitrary", "arbitrary"), vmem_limit_bytes=VMEM_LIMIT_BYTES),
        name="hybrid_mixer",
    )(x, x, g, win, poolw, pscale, lbl, hnorm, wpp, whp, wout)


def kernel(x, ffn1_norm, ffn1_w_gate, ffn1_w_up, ffn1_w_down, mix_norm, w_in, pool_w, pool_scale,
           lb_logits, hgrn_norm, w_pool_proj, w_hgrn_proj, w_out, ffn2_norm, ffn2_w_gate, ffn2_w_up,
           ffn2_w_down, final_norm):
    batch, seq, _ = x.shape
    depth = ffn1_norm.shape[0]
    xf = x.reshape(batch * seq, D_MODEL)
    fg = final_norm.reshape(1, D_MODEL)
    for l in range(depth):
        xf = _ffn(xf, ffn1_norm[l].reshape(1, D_MODEL), ffn1_w_gate, ffn1_w_up, ffn1_w_down, fg, l, False)
        xf = _mixer(xf, mix_norm[l].reshape(1, D_MODEL), w_in, pool_w[l],
                    pool_scale[l].reshape(1, POOL_WIDTH), lb_logits, hgrn_norm[l].reshape(1, HGRN_W),
                    w_pool_proj, w_hgrn_proj, w_out, l, batch)
        xf = _ffn(xf, ffn2_norm[l].reshape(1, D_MODEL), ffn2_w_gate, ffn2_w_up, ffn2_w_down, fg, l,
                  l == depth - 1)
    return xf.reshape(batch, seq, D_MODEL)
```

```python
import functools

import jax
import jax.numpy as jnp
from jax import lax
from jax.experimental import pallas as pl
from jax.experimental.pallas import tpu as pltpu

D_MODEL = 1024
D_FF = 2816
CHUNK = 64
SUB = 16
N_SUB = CHUNK // SUB
POOL_WIDTH = 512
POOL_WINDOWS = (2, 4, 8, 16)
POOL_GROUP_DIM = 128
POOL_CARRY = 16
HEADS = 4
HEAD_DIM = 128
HGRN_W = HEADS * HEAD_DIM
EPS = 1e-6
LOG2E = 1.4426950408889634

FF_TILE = 256
N_FF_TILES = D_FF // FF_TILE
TM_FFN = 512
TM_MIX = 512
CAST_ROWS_IN_PROJ = 64
CAST_ROWS_SQUARE = 256
VMEM_LIMIT_BYTES = 56 * 1024 * 1024

OFF_POOL = 0
OFF_Q = POOL_WIDTH
OFF_F = OFF_Q + HGRN_W
OFF_I = OFF_F + HGRN_W
OFF_OG = OFF_I + HGRN_W
OFF_GA = OFF_OG + HGRN_W
OFF_GB = OFF_GA + D_MODEL
D_IN = OFF_GB + D_MODEL

F32 = jnp.float32
BF16 = jnp.bfloat16


def _dot(a, b):
    return jnp.dot(a, b, preferred_element_type=F32)


def _dot_nt(a, b):
    return lax.dot_general(a, b, (((1,), (1,)), ((), ())), preferred_element_type=F32)


def _dot_tn(a, b):
    return lax.dot_general(a, b, (((0,), (0,)), ((), ())), preferred_element_type=F32)


def _rmsnorm(x, g):
    ms = jnp.mean(x * x, axis=-1, keepdims=True)
    return x * lax.rsqrt(ms + EPS) * g


def _sigmoid(x):
    return 1.0 / (1.0 + jnp.exp(-x))


def _weight_chunk_copy(src_hbm, layer, stage_ref, sem_ref, c):
    rows = stage_ref.shape[1]
    return pltpu.make_async_copy(src_hbm.at[layer, pl.ds(c * rows, rows), :], stage_ref.at[c % 2],
                                 sem_ref.at[c % 2])


def _cast_weight(src_hbm, layer, dst_ref, stage_ref, sem_ref):
    rows = stage_ref.shape[1]
    n_chunks, rem = divmod(dst_ref.shape[-2], rows)
    assert rem == 0
    _weight_chunk_copy(src_hbm, layer, stage_ref, sem_ref, 0).start()
    for c in range(n_chunks):
        if c + 1 < n_chunks:
            _weight_chunk_copy(src_hbm, layer, stage_ref, sem_ref, c + 1).start()
        _weight_chunk_copy(src_hbm, layer, stage_ref, sem_ref, c).wait()
        if len(dst_ref.shape) == 2:
            dst_ref[c * rows:(c + 1) * rows, :] = stage_ref[c % 2].astype(BF16)
        else:
            width = dst_ref.shape[2]
            for t in range(dst_ref.shape[0]):
                dst_ref[t, c * rows:(c + 1) * rows, :] = stage_ref[c % 2, :, t * width:(t + 1) * width].astype(BF16)


def _ffn_kernel(x_ref, g_ref, wg_hbm, wu_hbm, wd_hbm, fg_ref, o_ref, h_ref, a_ref, wg_ref, wu_ref, wd_ref,
                stage_in_ref, stage_out_ref, sem_ref, *, layer, final_norm):
    def tile_copies(j):
        k = j % 2
        cols = pl.ds(j * FF_TILE, FF_TILE)
        return (pltpu.make_async_copy(wg_hbm.at[layer, :, cols], stage_in_ref.at[0, k], sem_ref.at[0, k]),
                pltpu.make_async_copy(wu_hbm.at[layer, :, cols], stage_in_ref.at[1, k], sem_ref.at[1, k]),
                pltpu.make_async_copy(wd_hbm.at[layer, cols, :], stage_out_ref.at[k], sem_ref.at[2, k]))

    def fetch_tile(j):
        for copy in tile_copies(j):
            copy.start()

    def convert_tile(j):
        cols = slice(j * FF_TILE, (j + 1) * FF_TILE)
        if j + 1 < N_FF_TILES:
            fetch_tile(j + 1)
        for copy in tile_copies(j):
            copy.wait()
        wg_ref[:, cols] = stage_in_ref[0, j % 2].astype(BF16)
        wu_ref[:, cols] = stage_in_ref[1, j % 2].astype(BF16)
        wd_ref[cols, :] = stage_out_ref[j % 2].astype(BF16)

    def half_step(before_tile):
        x = x_ref[...]
        h_ref[...] = _rmsnorm(x, g_ref[...]).astype(BF16)
        for j in range(N_FF_TILES):
            before_tile(j)
            cols = slice(j * FF_TILE, (j + 1) * FF_TILE)
            h = h_ref[...]
            g = _dot(h, wg_ref[:, cols])
            u = _dot(h, wu_ref[:, cols])
            a_ref[:, cols] = (g * _sigmoid(g) * u).astype(BF16)
        y = x + 0.5 * _dot(a_ref[...], wd_ref[...])
        if final_norm:
            y = _rmsnorm(y, fg_ref[...])
        o_ref[...] = y

    @pl.when(pl.program_id(0) == 0)
    def _():
        fetch_tile(0)
        half_step(convert_tile)

    @pl.when(pl.program_id(0) > 0)
    def _():
        half_step(lambda j: None)


def _const_spec(shape):
    nd = len(shape)
    return pl.BlockSpec(shape, lambda *_: (0,) * nd, pipeline_mode=pl.Buffered(1))


def _ffn(x, g, wg, wu, wd, fg, layer, final_norm):
    n = x.shape[0]
    hbm = pl.BlockSpec(memory_space=pl.ANY)
    return pl.pallas_call(
        functools.partial(_ffn_kernel, layer=layer, final_norm=final_norm),
        grid=(n // TM_FFN,),
        in_specs=[
            pl.BlockSpec((TM_FFN, D_MODEL), lambda i: (i, 0)),
            _const_spec((1, D_MODEL)),
            hbm, hbm, hbm,
            _const_spec((1, D_MODEL)),
        ],
        out_specs=pl.BlockSpec((TM_FFN, D_MODEL), lambda i: (i, 0)),
        out_shape=jax.ShapeDtypeStruct((n, D_MODEL), F32),
        scratch_shapes=[
            pltpu.VMEM((TM_FFN, D_MODEL), BF16),
            pltpu.VMEM((TM_FFN, D_FF), BF16),
            pltpu.VMEM((D_MODEL, D_FF), BF16),
            pltpu.VMEM((D_MODEL, D_FF), BF16),
            pltpu.VMEM((D_FF, D_MODEL), BF16),
            pltpu.VMEM((2, 2, D_MODEL, FF_TILE), F32),
            pltpu.VMEM((2, FF_TILE, D_MODEL), F32),
            pltpu.SemaphoreType.DMA((3, 2)),
        ],
        compiler_params=pltpu.CompilerParams(
            dimension_semantics=("arbitrary",), vmem_limit_bytes=VMEM_LIMIT_BYTES),
        name="swiglu_half_step",
    )(x, g, wg, wu, wd, fg)


def _lower_bound(lbl_ref, layer):
    logits = lbl_ref[...]
    m = jnp.max(logits, axis=0, keepdims=True)
    e = jnp.exp(logits - m)
    p = e / jnp.sum(e, axis=0, keepdims=True)
    lb = jnp.zeros((1, HGRN_W), F32)
    for j in range(1, layer + 1):
        lb = lb + p[j:j + 1, :]
    return lb


def _head(hd):
    return slice(hd * HEAD_DIM, (hd + 1) * HEAD_DIM)


def _sub_scan(lf2):
    n8 = CHUNK // 8
    w = lf2.reshape(n8, 8, HGRN_W)
    sub8 = lax.broadcasted_iota(jnp.int32, (n8, 8, HGRN_W), 1)
    for d in (1, 2, 4):
        w = w + jnp.where(sub8 >= d, pltpu.roll(w, d, axis=1), 0.0)
    w = w.reshape(CHUNK, HGRN_W)
    parts = []
    for j in range(N_SUB):
        lo = w[SUB * j:SUB * j + 8, :]
        hi = w[SUB * j + 8:SUB * j + 16, :] + lo[7:8, :]
        parts += [lo, hi]
    return jnp.concatenate(parts, axis=0)


def _hgrn_chunk_head(r0, hd, c2_ref, lk_ref, q_ref, v_ref, st_ref, e_tot, e_pre, decay):
    hs = _head(hd)
    rows = slice(r0, r0 + CHUNK)
    c2 = c2_ref[hd, rows, :]
    lk = lk_ref[hd, rows, :]
    q = q_ref[hd, rows, :]
    v_b = v_ref[hd, rows, :].astype(BF16)

    def sub(a, j):
        return a[SUB * j:SUB * (j + 1), :]

    tot = [c2[SUB * j + SUB - 1:SUB * j + SUB, :] for j in range(N_SUB)]
    tot_own = jnp.concatenate([jnp.broadcast_to(t, (SUB, HEAD_DIM)) for t in tot], axis=0)
    qh = q * jnp.exp2(c2)
    kh = jnp.exp2(lk + tot_own)
    et = [None] + [e_tot[j][:, hs] for j in range(1, N_SUB)]
    qg = jnp.concatenate([sub(qh, 0)] + [sub(qh, j) * e_pre[j][:, hs] for j in range(1, N_SUB)],
                         axis=0).astype(BF16)
    qh_b = qh.astype(BF16)

    zero = jnp.zeros((SUB, HEAD_DIM), F32)
    k2_0 = sub(kh, 0) * et[1]
    k3_0 = k2_0 * et[2]
    k3_1 = sub(kh, 1) * et[2]
    k1 = jnp.concatenate([sub(kh, 0), zero, zero, zero], axis=0).astype(BF16)
    k2 = jnp.concatenate([k2_0, sub(kh, 1), zero, zero], axis=0).astype(BF16)
    k3 = jnp.concatenate([k3_0, k3_1, sub(kh, 2), zero], axis=0).astype(BF16)
    kg = jnp.concatenate([k3_0 * et[3], k3_1 * et[3], sub(kh, 2) * et[3], sub(kh, 3)],
                         axis=0).astype(BF16)

    a1 = _dot_nt(qh_b[SUB:2 * SUB, :], k1)
    a2 = _dot_nt(qh_b[2 * SUB:3 * SUB, :], k2)
    a3 = _dot_nt(qh_b[3 * SUB:4 * SUB, :], k3)
    a_cross = jnp.concatenate([jnp.zeros((SUB, CHUNK), F32), a1, a2, a3], axis=0)
    return hs, c2, q, v_b, qg, kg, a_cross


def _hgrn_chunk_head_finish(r0, hd, lk_ref, st_ref, decay, prepared):
    hs, c2, q, v_b, qg, kg, a_cross = prepared

    lane = lax.broadcasted_iota(jnp.int32, (8, CHUNK), 1)
    row8 = lax.broadcasted_iota(jnp.int32, (8, CHUNK), 0)
    blocks = []
    for j in range(N_SUB):
        base = r0 + SUB * j
        c_lo, c_hi = c2[SUB * j:SUB * j + 8, :], c2[SUB * j + 8:SUB * j + 16, :]
        q_lo, q_hi = q[SUB * j:SUB * j + 8, :], q[SUB * j + 8:SUB * j + 16, :]
        d_lo = jnp.zeros((8, CHUNK), F32)
        d_hi = jnp.zeros((8, CHUNK), F32)
        for s in range(SUB):
            lks = jnp.broadcast_to(lk_ref[hd, base + s:base + s + 1, :], (8, HEAD_DIM))
            here = lane == SUB * j + s
            col_hi = jnp.sum(q_hi * jnp.exp2(c_hi + lks), axis=-1, keepdims=True)
            d_hi = jnp.where(here, col_hi, d_hi)
            if s < 8:
                col_lo = jnp.sum(q_lo * jnp.exp2(c_lo + lks), axis=-1, keepdims=True)
                d_lo = jnp.where(here, col_lo, d_lo)
        d_lo = jnp.where(lane <= SUB * j + row8, d_lo, 0.0)
        d_hi = jnp.where(lane <= SUB * j + 8 + row8, d_hi, 0.0)
        blocks += [d_lo, d_hi]
    a = (a_cross + jnp.concatenate(blocks, axis=0)).astype(BF16)
    o_intra = _dot(a, v_b)

    st = st_ref[hd]
    o_state = _dot_nt(qg, st.astype(BF16))
    st_ref[hd] = decay[:, hs] * st + _dot_tn(v_b, kg)
    return o_state + o_intra


def _mixer_kernel(x_ref, xn_ref, g_ref, win_hbm, poolw_ref, pscale_ref, lbl_ref, hnorm_ref, wpp_hbm, whp_hbm,
                  wout_hbm, o_ref, h_ref, fz_ref, qz_ref, vz_ref, c2_ref, lk_ref, q_ref, v_ref, dec_ref, ext_ref,
                  hg_ref, pm_ref, og_ref, ga_ref, gb_ref, st_ref, win_ref, wpp_ref, whp_ref, wout_ref,
                  stage_in_ref, stage_sq_ref, sem_ref, *, layer):
    tb = pl.program_id(1)
    n_chunks = TM_MIX // CHUNK

    @pl.when((pl.program_id(0) == 0) & (tb == 0))
    def _():
        _cast_weight(win_hbm, layer, win_ref, stage_in_ref, sem_ref)
        _cast_weight(wpp_hbm, layer, wpp_ref, stage_sq_ref, sem_ref)
        _cast_weight(whp_hbm, layer, whp_ref, stage_sq_ref, sem_ref)
        _cast_weight(wout_hbm, layer, wout_ref, stage_sq_ref, sem_ref)

    lb = _lower_bound(lbl_ref, layer)
    log_lb = jnp.log(lb)
    log1m_lb = jnp.log1p(-lb)

    def tile(t):
        return slice(t * FF_TILE, (t + 1) * FF_TILE)

    def norm_job(src_ref, slot):
        def run():
            h_ref[slot] = _rmsnorm(src_ref[...], g_ref[...]).astype(BF16)
        return run

    def stage_job(dst_ref, off, t, slot):
        def run():
            dst_ref[:, tile(t)] = _dot(h_ref[slot], win_ref[off // FF_TILE + t])
        return run

    def stage_jobs(slot):
        return [stage_job(dst_ref, off, t, slot)
                for dst_ref, off in ((fz_ref, OFF_F), (qz_ref, OFF_Q), (vz_ref, OFF_I))
                for t in range(HGRN_W // FF_TILE)]

    def gates_job(ci):
        def run():
            rows = slice(ci * CHUNK, (ci + 1) * CHUNK)
            f = fz_ref[rows, :]
            l1p = jnp.log(1.0 + jnp.exp(-jnp.abs(f)))
            bb = log1m_lb + (jnp.minimum(f, 0.0) - l1p)
            lf2 = (jnp.maximum(log_lb, bb) + jnp.log(1.0 + jnp.exp(-jnp.abs(log_lb - bb)))) * LOG2E
            lk2 = (log1m_lb - jnp.maximum(f, 0.0) - l1p) * LOG2E
            c2 = _sub_scan(lf2)
            lk = lk2 - c2
            qc = qz_ref[rows, :]
            q = qc * _sigmoid(qc)
            vc = vz_ref[rows, :]
            for hd in range(HEADS):
                c2_ref[hd, rows, :] = c2[:, _head(hd)]
                lk_ref[hd, rows, :] = lk[:, _head(hd)]
                q_ref[hd, rows, :] = q[:, _head(hd)]
                v_ref[hd, rows, :] = vc[:, _head(hd)]
            tot = [c2[SUB * j + SUB - 1:SUB * j + SUB, :] for j in range(N_SUB)]
            p1 = tot[0]
            p2 = p1 + tot[1]
            p3 = p2 + tot[2]
            dec_ref[ci] = jnp.exp2(jnp.concatenate(
                [tot[1], tot[2], tot[3], p1, p2, p3, p3 + tot[3], p3 + tot[3]], axis=0))
        return run

    @pl.when(tb == 0)
    def _():
        st_ref[...] = jnp.zeros_like(st_ref)
        ext_ref[0:POOL_CARRY, :] = jnp.zeros((POOL_CARRY, POOL_WIDTH), F32)
        for job in [norm_job(x_ref, 0)] + stage_jobs(0) + [gates_job(ci) for ci in range(n_chunks)]:
            job()

    @pl.when(tb > 0)
    def _():
        h_ref[0] = h_ref[1]

    def proj(col0):
        return _dot(h_ref[0], win_ref[col0 // FF_TILE])

    def pool_proj_job(t):
        def run():
            ext_ref[POOL_CARRY:POOL_CARRY + TM_MIX, tile(t)] = proj(OFF_POOL + t * FF_TILE)
        return run

    def og_job(t):
        def run():
            og = proj(OFF_OG + t * FF_TILE)
            og_ref[:, tile(t)] = og * _sigmoid(og)
        return run

    def gate_job(dst_ref, off, t, act):
        def run():
            dst_ref[:, tile(t)] = act(proj(off + t * FF_TILE))
        return run

    def pool_mix_job(gi):
        def run():
            w = POOL_WINDOWS[gi]
            cols = slice(gi * POOL_GROUP_DIM, (gi + 1) * POOL_GROUP_DIM)
            pos = lax.broadcasted_iota(jnp.int32, (POOL_CARRY, POOL_GROUP_DIM), 0) + (tb * TM_MIX + 1)
            ext = ext_ref[:, cols]
            wsum = ext
            d = 1
            while d < w:
                wsum = wsum + pltpu.roll(wsum, d, axis=0)
                d *= 2
            wsum = wsum[POOL_CARRY:, :]
            u = ext[POOL_CARRY:, :]
            inv_head = 1.0 / jnp.minimum(pos, w).astype(F32)
            mean = jnp.concatenate([wsum[:POOL_CARRY] * inv_head, wsum[POOL_CARRY:] * (1.0 / w)], axis=0)
            mixed = _dot((mean - u).astype(BF16), poolw_ref[gi].astype(BF16))
            pm_ref[:, cols] = (mixed * pscale_ref[:, cols]).astype(BF16)
        return run

    def pool_merge_job():
        ga_ref[...] = ga_ref[...] * _dot(pm_ref[...], wpp_ref[...])
        ext_ref[0:POOL_CARRY, :] = ext_ref[TM_MIX:TM_MIX + POOL_CARRY, :]

    og_jobs = [og_job(t) for t in range(HGRN_W // FF_TILE)]
    pool_proj_jobs = [pool_proj_job(t) for t in range(POOL_WIDTH // FF_TILE)]
    ga_jobs = [gate_job(ga_ref, OFF_GA, t, _sigmoid) for t in range(D_MODEL // FF_TILE)]
    gb_jobs = [gate_job(gb_ref, OFF_GB, t, lambda z: z) for t in range(D_MODEL // FF_TILE)]
    mix_jobs = [pool_mix_job(gi) for gi in range(len(POOL_WINDOWS))]
    nxt = stage_jobs(1)
    schedule = [
        og_jobs + [norm_job(xn_ref, 1)] + nxt[0:1],
        nxt[1:4],
        nxt[4:6] + pool_proj_jobs[0:1],
        pool_proj_jobs[1:2] + ga_jobs[0:2] + [gates_job(0)],
        ga_jobs[2:4] + gb_jobs[0:1] + [gates_job(1), gates_job(2)],
        gb_jobs[1:3] + mix_jobs[0:2] + [gates_job(3)],
        gb_jobs[3:4] + mix_jobs[2:4] + [pool_merge_job, gates_job(4)],
        [gates_job(5)],
    ]
    assert len(schedule) == n_chunks
    gates_after_loop = (n_chunks - 2, n_chunks - 1)

    def finish_chunk(r0, decay, prepared):
        outs = [_hgrn_chunk_head_finish(r0, hd, lk_ref, st_ref, decay, prepared[hd]) for hd in range(HEADS)]
        normed = []
        for oh in outs:
            ms = jnp.mean(oh * oh, axis=-1, keepdims=True)
            normed.append(oh * lax.rsqrt(ms + EPS))
        on = jnp.concatenate(normed, axis=-1) * hnorm_ref[...]
        hg_ref[r0:r0 + CHUNK, :] = (on * og_ref[r0:r0 + CHUNK, :]).astype(BF16)

    waiting = None
    for ci in range(n_chunks):
        for job in schedule[ci]:
            job()
        r0 = ci * CHUNK
        dec = dec_ref[ci]
        e_tot = [None] + [dec[j - 1:j, :] for j in range(1, N_SUB)]
        e_pre = [None] + [dec[N_SUB - 2 + j:N_SUB - 1 + j, :] for j in range(1, N_SUB)]
        decay = dec[2 * N_SUB - 2:2 * N_SUB - 1, :]
        prepared = [_hgrn_chunk_head(r0, hd, c2_ref, lk_ref, q_ref, v_ref, st_ref, e_tot, e_pre, decay)
                    for hd in range(HEADS)]
        if waiting is not None:
            finish_chunk(*waiting)
        waiting = (r0, decay, prepared)
    finish_chunk(*waiting)

    for k in gates_after_loop:
        gates_job(k)()
    pb = _dot(hg_ref[...], whp_ref[...])
    merged = (ga_ref[...] + _sigmoid(gb_ref[...]) * pb).astype(BF16)
    o_ref[...] = x_ref[...] + _dot(merged, wout_ref[...])


def _mixer(x, g, win, poolw, pscale, lbl, hnorm, wpp, whp, wout, layer, batch):
    n = x.shape[0]
    nt = n // batch // TM_MIX
    depth = lbl.shape[0]
    hbm = pl.BlockSpec(memory_space=pl.ANY)
    return pl.pallas_call(
        functools.partial(_mixer_kernel, layer=layer),
        grid=(batch, nt),
        in_specs=[
            pl.BlockSpec((TM_MIX, D_MODEL), lambda b, t: (b * nt + t, 0)),
            pl.BlockSpec((TM_MIX, D_MODEL), lambda b, t: (b * nt + jnp.minimum(t + 1, nt - 1), 0)),
            _const_spec((1, D_MODEL)),
            hbm,
            _const_spec((len(POOL_WINDOWS), POOL_GROUP_DIM, POOL_GROUP_DIM)),
            _const_spec((1, POOL_WIDTH)),
            _const_spec((depth, HGRN_W)),
            _const_spec((1, HGRN_W)),
            hbm, hbm, hbm,
        ],
        out_specs=pl.BlockSpec((TM_MIX, D_MODEL), lambda b, t: (b * nt + t, 0)),
        out_shape=jax.ShapeDtypeStruct((n, D_MODEL), F32),
        scratch_shapes=[
            pltpu.VMEM((2, TM_MIX, D_MODEL), BF16),
            pltpu.VMEM((TM_MIX, HGRN_W), F32),
            pltpu.VMEM((TM_MIX, HGRN_W), F32),
            pltpu.VMEM((TM_MIX, HGRN_W), F32),
            pltpu.VMEM((HEADS, TM_MIX, HEAD_DIM), F32),
            pltpu.VMEM((HEADS, TM_MIX, HEAD_DIM), F32),
            pltpu.VMEM((HEADS, TM_MIX, HEAD_DIM), F32),
            pltpu.VMEM((HEADS, TM_MIX, HEAD_DIM), F32),
            pltpu.VMEM((TM_MIX // CHUNK, 8, HGRN_W), F32),
            pltpu.VMEM((POOL_CARRY + TM_MIX, POOL_WIDTH), F32),
            pltpu.VMEM((TM_MIX, HGRN_W), BF16),
            pltpu.VMEM((TM_MIX, POOL_WIDTH), BF16),
            pltpu.VMEM((TM_MIX, HGRN_W), F32),
            pltpu.VMEM((TM_MIX, D_MODEL), F32),
            pltpu.VMEM((TM_MIX, D_MODEL), F32),
            pltpu.VMEM((HEADS, HEAD_DIM, HEAD_DIM), F32),
            pltpu.VMEM((D_IN // FF_TILE, D_MODEL, FF_TILE), BF16),
            pltpu.VMEM((POOL_WIDTH, D_MODEL), BF16),
            pltpu.VMEM((HGRN_W, D_MODEL), BF16),
            pltpu.VMEM((D_MODEL, D_MODEL), BF16),
            pltpu.VMEM((2, CAST_ROWS_IN_PROJ, D_IN), F32),
            pltpu.VMEM((2, CAST_ROWS_SQUARE, D_MODEL), F32),
            pltpu.SemaphoreType.DMA((2,)),
        ],
        compiler_params=pltpu.CompilerParams(
            dimension_semantics=("arbitrary", "arbitrary"), vmem_limit_bytes=VMEM_LIMIT_BYTES),
        name="hybrid_mixer",
    )(x, x, g, win, poolw, pscale, lbl, hnorm, wpp, whp, wout)


def kernel(x, ffn1_norm, ffn1_w_gate, ffn1_w_up, ffn1_w_down, mix_norm, w_in, pool_w, pool_scale,
           lb_logits, hgrn_norm, w_pool_proj, w_hgrn_proj, w_out, ffn2_norm, ffn2_w_gate, ffn2_w_up,
           ffn2_w_down, final_norm):
    batch, seq, _ = x.shape
    depth = ffn1_norm.shape[0]
    xf = x.reshape(batch * seq, D_MODEL)
    fg = final_norm.reshape(1, D_MODEL)
    for l in range(depth):
        xf = _ffn(xf, ffn1_norm[l].reshape(1, D_MODEL), ffn1_w_gate, ffn1_w_up, ffn1_w_down, fg, l, False)
        xf = _mixer(xf, mix_norm[l].reshape(1, D_MODEL), w_in, pool_w[l],
                    pool_scale[l].reshape(1, POOL_WIDTH), lb_logits, hgrn_norm[l].reshape(1, HGRN_W),
                    w_pool_proj, w_hgrn_proj, w_out, l, batch)
        xf = _ffn(xf, ffn2_norm[l].reshape(1, D_MODEL), ffn2_w_gate, ffn2_w_up, ffn2_w_down, fg, l,
                  l == depth - 1)
    return xf.reshape(batch, seq, D_MODEL)
```
